```python
import math
import jax, jax.numpy as jnp
from jax import lax
import numpy as np

D_MODEL = 1024
BATCH = 2
SEQ = 8192
DEPTH = 4

N_MEM = 256
N_A_LAYERS = DEPTH // 2
N_B_LAYERS = DEPTH - N_A_LAYERS
QBLOCK = 128

TOK_HEADS = 12
HEAD_DIM = 64
MEM_HEADS = 4
MEM_HEAD_DIM = 64
TOK_WIDTH = TOK_HEADS * HEAD_DIM
MEM_WIDTH = MEM_HEADS * MEM_HEAD_DIM
MIX_WIDTH = TOK_WIDTH + MEM_WIDTH

MLA_Q_RANK = 256
MLA_KV_RANK = 128
MLA_NOPE = 64
MLA_ROPE = 32
MLA_V = HEAD_DIM
ROPE_THETA = 10000.0
MLA_IN = MLA_Q_RANK + MLA_KV_RANK + MLA_ROPE + MEM_WIDTH

NSA_GROUPS = 2
NSA_HPG = TOK_HEADS // NSA_GROUPS
NSA_BRANCHES = 3
CMP_LEN = 32
CMP_STRIDE = 16
CMP_HIDDEN = 256
SEL_LEN = 64
SEL_TOPK = 16
WINDOW = 512
NSA_IN = TOK_WIDTH + TOK_HEADS * NSA_BRANCHES + MEM_WIDTH
NSA_KV = NSA_BRANCHES * 2 * NSA_GROUPS * HEAD_DIM

D_FF = 2816

DN_ALPHA = (2 * DEPTH) ** 0.25
DN_BETA = (8 * DEPTH) ** -0.25

LN_EPS = 1e-5
RMS_EPS = 1e-6
NEG = -1e30
FORCE_BONUS = 1e4

kernel_name = "yoco_mla_nsa_macaron_deepnorm_mem"


def layer_norm(x, g, b):
    xf = x.astype(jnp.float32)
    mu = jnp.mean(xf, -1, keepdims=True)
    var = jnp.mean(jnp.square(xf - mu), -1, keepdims=True)
    return ((xf - mu) * lax.rsqrt(var + LN_EPS) * g + b).astype(x.dtype)


def rms_norm(x, g):
    xf = x.astype(jnp.float32)
    return (xf * lax.rsqrt(jnp.mean(xf * xf, -1, keepdims=True) + RMS_EPS) * g).astype(x.dtype)


def masked_softmax(s, mask, axis=-1):
    s = jnp.where(mask, s.astype(jnp.float32), NEG)
    m = jnp.max(s, axis=axis, keepdims=True)
    e = jnp.exp(s - m) * mask
    return e / jnp.maximum(jnp.sum(e, axis=axis, keepdims=True), 1e-30)


def swiglu(x, w_gu, w_down):
    g, u = jnp.split(x @ w_gu, 2, axis=-1)
    return (jax.nn.silu(g) * u) @ w_down


def rope(x, pos):
    half = x.shape[-1] // 2
    freq = ROPE_THETA ** (-jnp.arange(half, dtype=jnp.float32) / half)
    ang = pos.astype(jnp.float32)[:, None] * freq[None, :]
    cos, sin = jnp.cos(ang), jnp.sin(ang)
    x1, x2 = x[..., :half], x[..., half:]
    return jnp.concatenate([x1 * cos - x2 * sin, x2 * cos + x1 * sin], -1).astype(x.dtype)


def alibi_slopes(n):
    def pow2(k):
        start = 2.0 ** (-8.0 / k)
        return [start ** (i + 1) for i in range(k)]
    if math.log2(n).is_integer():
        s = pow2(n)
    else:
        c = 2 ** math.floor(math.log2(n))
        s = pow2(c) + pow2(2 * c)[0::2][: n - c]
    return np.asarray(s, np.float32)


def memory_attention(q_mem, mem, w_mem_kv):
    B, T, _ = q_mem.shape
    M = mem.shape[1]
    k, v = jnp.split(mem @ w_mem_kv, 2, axis=-1)
    q = q_mem.reshape(B, T, MEM_HEADS, MEM_HEAD_DIM)
    k = k.reshape(B, M, MEM_HEADS, MEM_HEAD_DIM)
    v = v.reshape(B, M, MEM_HEADS, MEM_HEAD_DIM)
    s = jnp.einsum('bthd,bmhd->bhtm', q, k).astype(jnp.float32) * (MEM_HEAD_DIM ** -0.5)
    p = jax.nn.softmax(s, axis=-1).astype(v.dtype)
    return jnp.einsum('bhtm,bmhd->bthd', p, v).reshape(B, T, MEM_WIDTH)


def mla_mix(x, w_in, q_norm_g, kv_norm_g, w_uq, w_ukv, pos):
    B, T, _ = x.shape
    H = TOK_HEADS
    c_q, c_kv, k_r, q_mem = jnp.split(
        x @ w_in, [MLA_Q_RANK, MLA_Q_RANK + MLA_KV_RANK, MLA_Q_RANK + MLA_KV_RANK + MLA_ROPE], axis=-1)
    q = (rms_norm(c_q, q_norm_g) @ w_uq).reshape(B, T, H, MLA_NOPE + MLA_ROPE).transpose(0, 2, 1, 3)
    q = jnp.concatenate([q[..., :MLA_NOPE], rope(q[..., MLA_NOPE:], pos)], -1)
    kv = (rms_norm(c_kv, kv_norm_g) @ w_ukv).reshape(B, T, H, MLA_NOPE + MLA_V).transpose(0, 2, 1, 3)
    k_nope, v = kv[..., :MLA_NOPE], kv[..., MLA_NOPE:]
    k_r = rope(k_r, pos)
    k = jnp.concatenate([k_nope, jnp.broadcast_to(k_r[:, None], (B, H, T, MLA_ROPE))], -1)
    scale = (MLA_NOPE + MLA_ROPE) ** -0.5
    key_pos = jnp.arange(T)

    def block(c):
        t0 = c * QBLOCK
        tq = t0 + jnp.arange(QBLOCK)
        qb = lax.dynamic_slice_in_dim(q, t0, QBLOCK, axis=2)
        s = jnp.einsum('bhqd,bhkd->bhqk', qb, k) * scale
        p = masked_softmax(s, key_pos[None, :] <= tq[:, None]).astype(v.dtype)
        return jnp.einsum('bhqk,bhkd->bqhd', p, v)

    o = lax.map(block, jnp.arange(T // QBLOCK))
    o = o.transpose(1, 0, 2, 3, 4).reshape(B, T, TOK_WIDTH)
    return o, q_mem


def nsa_shared_kv(h, w_kv, cmp_pos, cmp_w1, cmp_b1, cmp_w2):
    B, T, _ = h.shape
    G = NSA_GROUPS
    kv = (h @ w_kv).reshape(B, T, NSA_BRANCHES, 2, G, HEAD_DIM).transpose(2, 3, 0, 4, 1, 5)
    n_cmp = (T - CMP_LEN) // CMP_STRIDE + 1
    idx = np.arange(n_cmp)[:, None] * CMP_STRIDE + np.arange(CMP_LEN)[None, :]

    def compress(z, j):
        blocks = z[:, :, idx] + cmp_pos[j]
        flat = blocks.reshape(B, G, n_cmp, CMP_LEN * HEAD_DIM)
        return jax.nn.gelu(flat @ cmp_w1[j] + cmp_b1[j]) @ cmp_w2[j]

    kc = compress(kv[0, 0], 0)
    vc = compress(kv[0, 1], 1)
    n_sel = T // SEL_LEN
    ks = kv[1, 0].reshape(B, G, n_sel, SEL_LEN, HEAD_DIM)
    vs = kv[1, 1].reshape(B, G, n_sel, SEL_LEN, HEAD_DIM)
    pad = ((0, 0), (0, 0), (WINDOW, 0), (0, 0))
    kw_pad = jnp.pad(kv[2, 0], pad)
    vw_pad = jnp.pad(kv[2, 1], pad)
    return kc, vc, ks, vs, kw_pad, vw_pad


def nsa_mix(x, w_in, kc, vc, ks, vs, kw_pad, vw_pad, slopes):
    B, T, _ = x.shape
    G, HPG = NSA_GROUPS, NSA_HPG
    q, gates, q_mem = jnp.split(x @ w_in, [TOK_WIDTH, TOK_WIDTH + TOK_HEADS * NSA_BRANCHES], axis=-1)
    q = q.reshape(B, T, G, HPG, HEAD_DIM).transpose(0, 2, 3, 1, 4)
    gates = jax.nn.sigmoid(gates.reshape(B, T, G, HPG, NSA_BRANCHES).transpose(0, 2, 3, 1, 4))
    m = slopes.reshape(G, HPG)[None, :, :, None, None]
    n_cmp = kc.shape[2]
    n_sel = ks.shape[2]
    topk = min(SEL_TOPK, n_sel)
    cmp_start = np.arange(n_cmp) * CMP_STRIDE
    cmp_end = jnp.asarray(cmp_start + CMP_LEN - 1)
    cmp_center = jnp.asarray(cmp_start + (CMP_LEN - 1) * 0.5, dtype=jnp.float32)
    sel_start = np.arange(n_sel) * SEL_LEN
    overlap = np.clip(np.minimum(cmp_start[:, None] + CMP_LEN, sel_start[None, :] + SEL_LEN)
                      - np.maximum(cmp_start[:, None], sel_start[None, :]), 0, None)
    agg = jnp.asarray(overlap / CMP_LEN, dtype=jnp.float32)
    blk = jnp.arange(n_sel)
    scale = HEAD_DIM ** -0.5
    gather = jax.vmap(jax.vmap(lambda blocks, i: blocks[i]))

    def block(c):
        t0 = c * QBLOCK
        tq = t0 + jnp.arange(QBLOCK)
        qb = lax.dynamic_slice_in_dim(q, t0, QBLOCK, axis=3)
        gb = lax.dynamic_slice_in_dim(gates, t0, QBLOCK, axis=3)
        dist_c = tq[:, None].astype(jnp.float32) - cmp_center[None, :]
        s_c = jnp.einsum('bghqd,bgnd->bghqn', qb, kc) * scale - m * dist_c
        p_c = masked_softmax(s_c, cmp_end[None, :] <= tq[:, None])
        o_c = jnp.einsum('bghqn,bgnd->bghqd', p_c.astype(vc.dtype), vc)
        imp = jnp.einsum('bghqn,nj->bgqj', p_c, agg)
        cur = tq // SEL_LEN
        forced = (blk[None] == 0) | (blk[None] == cur[:, None]) | (blk[None] == cur[:, None] - 1)
        imp = jnp.where(forced, imp + FORCE_BONUS, imp)
        imp = jnp.where(blk[None] * SEL_LEN <= tq[:, None], imp, NEG)
        _, sel = lax.top_k(imp, topk)
        k_sel = gather(ks, sel)
        v_sel = gather(vs, sel)
        pos_s = sel[..., None] * SEL_LEN + jnp.arange(SEL_LEN)
        dist_s = (tq[None, None, :, None, None] - pos_s)[:, :, None]
        s_s = jnp.einsum('bghqd,bgqkld->bghqkl', qb, k_sel) * scale - m[..., None] * dist_s
        p_s = masked_softmax(s_s, dist_s >= 0, axis=(-2, -1))
        o_s = jnp.einsum('bghqkl,bgqkld->bghqd', p_s.astype(v_sel.dtype), v_sel)
        kw = lax.dynamic_slice_in_dim(kw_pad, t0, WINDOW + QBLOCK, axis=2)
        vw = lax.dynamic_slice_in_dim(vw_pad, t0, WINDOW + QBLOCK, axis=2)
        pos_w = t0 - WINDOW + jnp.arange(WINDOW + QBLOCK)
        dist_w = tq[:, None] - pos_w[None, :]
        mask_w = (dist_w >= 0) & (dist_w < WINDOW) & (pos_w[None, :] >= 0)
        s_w = jnp.einsum('bghqd,bgkd->bghqk', qb, kw) * scale - m * dist_w
        p_w = masked_softmax(s_w, mask_w)
        o_w = jnp.einsum('bghqk,bgkd->bghqd', p_w.astype(vw.dtype), vw)
        o = gb[..., 0:1] * o_c + gb[..., 1:2] * o_s + gb[..., 2:3] * o_w
        return o.transpose(0, 3, 1, 2, 4).reshape(B, QBLOCK, TOK_WIDTH)

    o = lax.map(block, jnp.arange(T // QBLOCK))
    o = o.transpose(1, 0, 2, 3).reshape(B, T, TOK_WIDTH)
    return o, q_mem


def setup_inputs(seed: int = 0) -> dict:
    key = jax.random.key(seed)
    ks = jax.random.split(key, 20)

    def nrm(k, shape, fan_in, gain=1.0):
        return jax.random.normal(k, shape, jnp.float32) * (gain * fan_in ** -0.5)

    def gain(k, shape):
        return 1.0 + 0.02 * jax.random.normal(k, shape, jnp.float32)

    return {
        "x": jax.random.normal(ks[0], (BATCH, SEQ, D_MODEL), jnp.float32),
        "mem": jax.random.normal(ks[1], (BATCH, N_MEM, D_MODEL), jnp.float32),
        "ln_g": gain(ks[2], (DEPTH, 3, D_MODEL)),
        "ln_b": 0.02 * jax.random.normal(ks[3], (DEPTH, 3, D_MODEL), jnp.float32),
        "ffn_w_gu": nrm(ks[4], (DEPTH, 2, D_MODEL, 2 * D_FF), D_MODEL),
        "ffn_w_down": nrm(ks[5], (DEPTH, 2, D_FF, D_MODEL), D_FF, DN_BETA),
        "w_mem_kv": nrm(ks[6], (DEPTH, D_MODEL, 2 * MEM_WIDTH), D_MODEL),
        "w_out": nrm(ks[7], (DEPTH, MIX_WIDTH, D_MODEL), MIX_WIDTH, DN_BETA),
        "mla_w_in": nrm(ks[8], (N_A_LAYERS, D_MODEL, MLA_IN), D_MODEL),
        "mla_q_norm_g": gain(ks[9], (N_A_LAYERS, MLA_Q_RANK)),
        "mla_kv_norm_g": gain(ks[10], (N_A_LAYERS, MLA_KV_RANK)),
        "mla_w_uq": nrm(ks[11], (N_A_LAYERS, MLA_Q_RANK, TOK_HEADS * (MLA_NOPE + MLA_ROPE)), MLA_Q_RANK),
        "mla_w_ukv": nrm(ks[12], (N_A_LAYERS, MLA_KV_RANK, TOK_HEADS * (MLA_NOPE + MLA_V)), MLA_KV_RANK),
        "nsa_w_in": nrm(ks[13], (N_B_LAYERS, D_MODEL, NSA_IN), D_MODEL),
        "nsa_w_kv": nrm(ks[14], (D_MODEL, NSA_KV), D_MODEL),
        "cmp_pos": 0.1 * jax.random.normal(ks[15], (2, CMP_LEN, HEAD_DIM), jnp.float32),
        "cmp_w1": nrm(ks[16], (2, CMP_LEN * HEAD_DIM, CMP_HIDDEN), CMP_LEN * HEAD_DIM),
        "cmp_b1": 0.02 * jax.random.normal(ks[17], (2, CMP_HIDDEN), jnp.float32),
        "cmp_w2": nrm(ks[18], (2, CMP_HIDDEN, HEAD_DIM), CMP_HIDDEN),
    }


def reference(x, mem, ln_g, ln_b, ffn_w_gu, ffn_w_down, w_mem_kv, w_out,
              mla_w_in, mla_q_norm_g, mla_kv_norm_g, mla_w_uq, mla_w_ukv,
              nsa_w_in, nsa_w_kv, cmp_pos, cmp_w1, cmp_b1, cmp_w2):
    T = x.shape[1]
    pos = jnp.arange(T)
    slopes = jnp.asarray(alibi_slopes(TOK_HEADS))
    shared = None
    for layer in range(DEPTH):
        x = layer_norm(DN_ALPHA * x + 0.5 * swiglu(x, ffn_w_gu[layer, 0], ffn_w_down[layer, 0]),
                       ln_g[layer, 0], ln_b[layer, 0])
        if layer < N_A_LAYERS:
            o_tok, q_mem = mla_mix(x, mla_w_in[layer], mla_q_norm_g[layer], mla_kv_norm_g[layer],
                                   mla_w_uq[layer], mla_w_ukv[layer], pos)
        else:
            b = layer - N_A_LAYERS
            kc, vc, ks_, vs_, kw_pad, vw_pad = shared
            o_tok, q_mem = nsa_mix(x, nsa_w_in[b], kc, vc, ks_, vs_, kw_pad, vw_pad, slopes)
        o_mem = memory_attention(q_mem, mem, w_mem_kv[layer])
        mix = jnp.concatenate([o_tok, o_mem], axis=-1) @ w_out[layer]
        x = layer_norm(DN_ALPHA * x + mix, ln_g[layer, 1], ln_b[layer, 1])
        x = layer_norm(DN_ALPHA * x + 0.5 * swiglu(x, ffn_w_gu[layer, 1], ffn_w_down[layer, 1]),
                       ln_g[layer, 2], ln_b[layer, 2])
        if layer == N_A_LAYERS - 1:
            shared = nsa_shared_kv(x, nsa_w_kv, cmp_pos, cmp_w1, cmp_b1, cmp_w2)
    return x
```

```python
import functools
import math

import numpy as np
import jax
import jax.numpy as jnp
from jax import lax
from jax.experimental import pallas as pl
from jax.experimental.pallas import tpu as pltpu

F32 = jnp.float32
BF16 = jnp.bfloat16

D_MODEL = 1024
DEPTH = 4
N_A_LAYERS = DEPTH // 2

TOK_HEADS = 12
HEAD_DIM = 64
MEM_HEADS = 4
MEM_HEAD_DIM = 64
TOK_WIDTH = TOK_HEADS * HEAD_DIM
MEM_WIDTH = MEM_HEADS * MEM_HEAD_DIM

MLA_Q_RANK = 256
MLA_KV_RANK = 128
MLA_NOPE = 64
MLA_ROPE = 32
ROPE_THETA = 10000.0

NSA_GROUPS = 2
NSA_HPG = TOK_HEADS // NSA_GROUPS
NSA_BRANCHES = 3
CMP_LEN = 32
CMP_STRIDE = 16
CMP_HIDDEN = 256
SEL_LEN = 64
SEL_SHIFT = 6
SEL_TOPK = 16
WINDOW = 512

D_FF = 2816
DN_ALPHA = (2 * DEPTH) ** 0.25
LN_EPS = 1e-5
RMS_EPS = 1e-6
NEG = -1e30
FORCE_BONUS = 1e4

LANES = 128
VMEM_LIMIT = 56 * 1024 * 1024


def _cparams(sem):
    return pltpu.CompilerParams(dimension_semantics=sem, vmem_limit_bytes=VMEM_LIMIT)


def _layer_norm(y, g, b):
    mu = jnp.mean(y, -1, keepdims=True)
    yc = y - mu
    var = jnp.mean(yc * yc, -1, keepdims=True)
    return yc * lax.rsqrt(var + LN_EPS) * g + b


def _rms_norm(x, g):
    return x * lax.rsqrt(jnp.mean(x * x, -1, keepdims=True) + RMS_EPS) * g


def _dot(a, b):
    return jnp.dot(a, b, preferred_element_type=F32)


def _dot_nt(a, b):
    return lax.dot_general(a, b, (((1,), (1,)), ((), ())), preferred_element_type=F32)


FFN_TM = 512
FFN_TF = 1408


def _ffn_kernel(x_ref, wg_ref, wu_ref, wd_ref, g_ref, b_ref, o_ref, acc_ref, *, nj):
    j = pl.program_id(1)
    xb = x_ref[...].astype(BF16)
    gate = _dot(xb, wg_ref[...])
    up = _dot(xb, wu_ref[...])
    h = (gate * jax.nn.sigmoid(gate) * up).astype(BF16)
    part = _dot(h, wd_ref[...])

    @pl.when(j == 0)
    def _():
        acc_ref[...] = part

    @pl.when(j > 0)
    def _():
        acc_ref[...] += part

    @pl.when(j == nj - 1)
    def _():
        y = DN_ALPHA * x_ref[...] + 0.5 * acc_ref[...]
        o_ref[...] = _layer_norm(y, g_ref[...], b_ref[...])


def _ffn_ln(x, w_gu, w_down, g, b):
    n = x.shape[0]
    nj = D_FF // FFN_TF
    return pl.pallas_call(
        functools.partial(_ffn_kernel, nj=nj),
        grid=(n // FFN_TM, nj),
        in_specs=[
            pl.BlockSpec((FFN_TM, D_MODEL), lambda i, j: (i, 0)),
            pl.BlockSpec((D_MODEL, FFN_TF), lambda i, j: (0, j)),
            pl.BlockSpec((D_MODEL, FFN_TF), lambda i, j: (0, j + nj)),
            pl.BlockSpec((FFN_TF, D_MODEL), lambda i, j: (j, 0)),
            pl.BlockSpec((1, D_MODEL), lambda i, j: (0, 0)),
            pl.BlockSpec((1, D_MODEL), lambda i, j: (0, 0)),
        ],
        out_specs=pl.BlockSpec((FFN_TM, D_MODEL), lambda i, j: (i, 0)),
        out_shape=jax.ShapeDtypeStruct((n, D_MODEL), F32),
        scratch_shapes=[pltpu.VMEM((FFN_TM, D_MODEL), F32)],
        compiler_params=_cparams(("parallel", "arbitrary")),
        name="ffn_ln",
    )(x, w_gu, w_gu, w_down, g, b)


def _memkv_kernel(mem_ref, w_ref, km_ref, v_ref):
    kv = _dot(mem_ref[0].astype(BF16), w_ref[...])
    k = kv[:, :MEM_WIDTH]
    v_ref[0] = kv[:, MEM_WIDTH:].astype(BF16)
    lane = lax.broadcasted_iota(jnp.int32, k.shape, 1)
    for h in range(MEM_HEADS):
        in_head = (lane >= h * MEM_HEAD_DIM) & (lane < (h + 1) * MEM_HEAD_DIM)
        km_ref[0, h] = jnp.where(in_head, k, 0.0).astype(BF16)


def _mem_kv(mem, w_mem_kv):
    bsz, m, _ = mem.shape
    return pl.pallas_call(
        _memkv_kernel,
        grid=(bsz,),
        in_specs=[
            pl.BlockSpec((1, m, D_MODEL), lambda b: (b, 0, 0)),
            pl.BlockSpec((D_MODEL, 2 * MEM_WIDTH), lambda b: (0, 0)),
        ],
        out_specs=[
            pl.BlockSpec((1, MEM_HEADS, m, MEM_WIDTH), lambda b: (b, 0, 0, 0)),
            pl.BlockSpec((1, m, MEM_WIDTH), lambda b: (b, 0, 0)),
        ],
        out_shape=[
            jax.ShapeDtypeStruct((bsz, MEM_HEADS, m, MEM_WIDTH), BF16),
            jax.ShapeDtypeStruct((bsz, m, MEM_WIDTH), BF16),
        ],
        compiler_params=_cparams(("parallel",)),
        name="mem_kv",
    )(mem, w_mem_kv)


OUT_TM = 512


def _mix_out_kernel(x_ref, ot_ref, qm_ref, km_ref, vm_ref, wt_ref, wm_ref, g_ref, b_ref, o_ref):
    qm = qm_ref[...]
    vm = vm_ref[0]
    lane = lax.broadcasted_iota(jnp.int32, (qm.shape[0], MEM_WIDTH), 1)
    o_mem = jnp.zeros((qm.shape[0], MEM_WIDTH), F32)
    for h in range(MEM_HEADS):
        s = _dot_nt(qm, km_ref[0, h]) * (MEM_HEAD_DIM ** -0.5)
        m = jnp.max(s, -1, keepdims=True)
        e = jnp.exp(s - m)
        p = (e / jnp.sum(e, -1, keepdims=True)).astype(BF16)
        pv = _dot(p, vm)
        in_head = (lane >= h * MEM_HEAD_DIM) & (lane < (h + 1) * MEM_HEAD_DIM)
        o_mem = jnp.where(in_head, pv, o_mem)
    mix = _dot(ot_ref[...], wt_ref[...]) + _dot(o_mem.astype(BF16), wm_ref[...])
    y = DN_ALPHA * x_ref[...] + mix
    o_ref[...] = _layer_norm(y, g_ref[...], b_ref[...])


def _mix_out(x, o_tok, q_mem, km, vm, w_tok, w_mem, g, b, seq):
    n = x.shape[0]
    kt = o_tok.shape[1]
    m = vm.shape[1]
    per_b = seq // OUT_TM
    return pl.pallas_call(
        _mix_out_kernel,
        grid=(n // OUT_TM,),
        in_specs=[
            pl.BlockSpec((OUT_TM, D_MODEL), lambda i: (i, 0)),
            pl.BlockSpec((OUT_TM, kt), lambda i: (i, 0)),
            pl.BlockSpec((OUT_TM, MEM_WIDTH), lambda i: (i, 0)),
            pl.BlockSpec((1, MEM_HEADS, m, MEM_WIDTH), lambda i: (i // per_b, 0, 0, 0)),
            pl.BlockSpec((1, m, MEM_WIDTH), lambda i: (i // per_b, 0, 0)),
            pl.BlockSpec((kt, D_MODEL), lambda i: (0, 0)),
            pl.BlockSpec((MEM_WIDTH, D_MODEL), lambda i: (0, 0)),
            pl.BlockSpec((1, D_MODEL), lambda i: (0, 0)),
            pl.BlockSpec((1, D_MODEL), lambda i: (0, 0)),
        ],
        out_specs=pl.BlockSpec((OUT_TM, D_MODEL), lambda i: (i, 0)),
        out_shape=jax.ShapeDtypeStruct((n, D_MODEL), F32),
        compiler_params=_cparams(("parallel",)),
        name="mix_out",
    )(x, o_tok, q_mem, km, vm, w_tok, w_mem, g, b)


MLA_TM = 512
MLA_IN_COLS = MLA_Q_RANK + MLA_KV_RANK + 2 * LANES + MEM_WIDTH
HQ = TOK_HEADS * LANES


def _mla_proj_kernel(x_ref, win_ref, qg_ref, kvg_ref, wq_ref, wk_ref, wv_ref, cos_ref, sin_ref,
                     q_ref, k_ref, v_ref, qm_ref):
    xb = x_ref[...].astype(BF16)
    hh = _dot(xb, win_ref[...])
    c_q = hh[:, :MLA_Q_RANK]
    c_kv = hh[:, MLA_Q_RANK:MLA_Q_RANK + MLA_KV_RANK]
    o = MLA_Q_RANK + MLA_KV_RANK
    kr = hh[:, o:o + LANES]
    kr_sw = hh[:, o + LANES:o + 2 * LANES]
    qm_ref[...] = hh[:, o + 2 * LANES:].astype(BF16)
    cos = cos_ref[...]
    sin = sin_ref[...]
    qq = _dot(_rms_norm(c_q, qg_ref[...]).astype(BF16), wq_ref[...])
    ckv = _rms_norm(c_kv, kvg_ref[...]).astype(BF16)
    kk = _dot(ckv, wk_ref[...])
    v_ref[...] = _dot(ckv, wv_ref[...]).astype(BF16)
    kr_rot = kr * cos + kr_sw * sin
    for h in range(TOK_HEADS):
        sl = slice(h * LANES, (h + 1) * LANES)
        sl2 = slice(HQ + h * LANES, HQ + (h + 1) * LANES)
        q_ref[:, sl] = (qq[:, sl] * cos + qq[:, sl2] * sin).astype(BF16)
        k_ref[:, sl] = (kk[:, sl] + kr_rot).astype(BF16)


def _mla_proj(x, w_in, qg, kvg, w_q, w_k, w_v, cos_t, sin_t, seq):
    n = x.shape[0]
    per_b = seq // MLA_TM
    full = lambda shape: pl.BlockSpec(shape, lambda i: (0, 0))
    return pl.pallas_call(
        _mla_proj_kernel,
        grid=(n // MLA_TM,),
        in_specs=[
            pl.BlockSpec((MLA_TM, D_MODEL), lambda i: (i, 0)),
            full((D_MODEL, MLA_IN_COLS)),
            full((1, MLA_Q_RANK)),
            full((1, MLA_KV_RANK)),
            full((MLA_Q_RANK, 2 * HQ)),
            full((MLA_KV_RANK, HQ)),
            full((MLA_KV_RANK, TOK_WIDTH)),
            pl.BlockSpec((MLA_TM, LANES), lambda i: (i % per_b, 0)),
            pl.BlockSpec((MLA_TM, LANES), lambda i: (i % per_b, 0)),
        ],
        out_specs=[
            pl.BlockSpec((MLA_TM, HQ), lambda i: (i, 0)),
            pl.BlockSpec((MLA_TM, HQ), lambda i: (i, 0)),
            pl.BlockSpec((MLA_TM, TOK_WIDTH), lambda i: (i, 0)),
            pl.BlockSpec((MLA_TM, MEM_WIDTH), lambda i: (i, 0)),
        ],
        out_shape=[
            jax.ShapeDtypeStruct((n, HQ), BF16),
            jax.ShapeDtypeStruct((n, HQ), BF16),
            jax.ShapeDtypeStruct((n, TOK_WIDTH), BF16),
            jax.ShapeDtypeStruct((n, MEM_WIDTH), BF16),
        ],
        compiler_params=_cparams(("parallel",)),
        name="mla_proj",
    )(x, w_in, qg, kvg, w_q, w_k, w_v, cos_t, sin_t)


MLA_TQ = 512
MLA_TK = 512
MLA_SCALE = (MLA_NOPE + MLA_ROPE) ** -0.5


def _mla_attn_kernel(q_ref, k_ref, v_ref, o_ref, m_ref, l_ref, acc_ref):
    i = pl.program_id(2)
    j = pl.program_id(3)

    @pl.when(j == 0)
    def _():
        m_ref[...] = jnp.full(m_ref.shape, NEG, F32)
        l_ref[...] = jnp.zeros(l_ref.shape, F32)
        acc_ref[...] = jnp.zeros(acc_ref.shape, F32)

    def tile(masked):
        v = v_ref[...]
        if masked:
            row = lax.broadcasted_iota(jnp.int32, (MLA_TQ, MLA_TK), 0)
            col = lax.broadcasted_iota(jnp.int32, (MLA_TQ, MLA_TK), 1)
            keep = col <= row
        for hd in range(2):
            sl = slice(hd * LANES, (hd + 1) * LANES)
            s = _dot_nt(q_ref[:, sl], k_ref[:, sl]) * MLA_SCALE
            if masked:
                s = jnp.where(keep, s, NEG)
            m_old = m_ref[hd]
            m_new = jnp.maximum(m_old, jnp.max(s, -1, keepdims=True))
            alpha = jnp.exp(m_old - m_new)
            p = jnp.exp(s - m_new)
            if masked:
                p = jnp.where(keep, p, 0.0)
            l_ref[hd] = alpha * l_ref[hd] + jnp.sum(p, -1, keepdims=True)
            acc_ref[hd] = alpha * acc_ref[hd] + _dot(p.astype(BF16), v)
            m_ref[hd] = m_new

    @pl.when(j < i)
    def _():
        tile(False)

    @pl.when(j == i)
    def _():
        tile(True)
        lane = lax.broadcasted_iota(jnp.int32, (MLA_TQ, LANES), 1)
        o0 = acc_ref[0] / jnp.maximum(l_ref[0], 1e-30)
        o1 = acc_ref[1] / jnp.maximum(l_ref[1], 1e-30)
        o_ref[...] = jnp.where(lane < HEAD_DIM, o0, o1).astype(BF16)


def _mla_attn(q, k, v, bsz, seq):
    assert MLA_TQ == MLA_TK
    nq = seq // MLA_TQ
    return pl.pallas_call(
        _mla_attn_kernel,
        grid=(bsz, TOK_HEADS // 2, nq, nq),
        in_specs=[
            pl.BlockSpec((MLA_TQ, 2 * LANES), lambda b, p, i, j: (b * nq + i, p)),
            pl.BlockSpec((MLA_TK, 2 * LANES), lambda b, p, i, j: (b * nq + jnp.minimum(i, j), p)),
            pl.BlockSpec((MLA_TK, LANES), lambda b, p, i, j: (b * nq + jnp.minimum(i, j), p)),
        ],
        out_specs=pl.BlockSpec((MLA_TQ, LANES), lambda b, p, i, j: (b * nq + i, p)),
        out_shape=jax.ShapeDtypeStruct((bsz * seq, TOK_WIDTH), BF16),
        scratch_shapes=[
            pltpu.VMEM((2, MLA_TQ, 1), F32),
            pltpu.VMEM((2, MLA_TQ, 1), F32),
            pltpu.VMEM((2, MLA_TQ, LANES), F32),
        ],
        compiler_params=_cparams(("parallel", "parallel", "parallel", "arbitrary")),
        name="mla_attn",
    )(q, k, v)


KV_TM = 512
NSA_CMP_COLS = 4 * LANES
NSA_KV_COLS = NSA_CMP_COLS + 4 * LANES


def _nsa_kv_kernel(x_ref, w_ref, zc_ref, ks_ref, vs_ref, kw_ref, vw_ref):
    y = _dot(x_ref[...].astype(BF16), w_ref[...])
    for c in range(4):
        zc_ref[c] = y[:, c * LANES:c * LANES + HEAD_DIM]
    o = NSA_CMP_COLS
    ks_ref[...] = y[:, o:o + LANES].astype(BF16)
    vs_ref[...] = y[:, o + LANES:o + 2 * LANES].astype(BF16)
    kw_ref[...] = y[:, o + 2 * LANES:o + 3 * LANES].astype(BF16)
    vw_ref[...] = y[:, o + 3 * LANES:o + 4 * LANES].astype(BF16)


def _nsa_kv(x, w):
    n = x.shape[0]
    tile = lambda: pl.BlockSpec((KV_TM, LANES), lambda i: (i, 0))
    return pl.pallas_call(
        _nsa_kv_kernel,
        grid=(n // KV_TM,),
        in_specs=[
            pl.BlockSpec((KV_TM, D_MODEL), lambda i: (i, 0)),
            pl.BlockSpec((D_MODEL, NSA_KV_COLS), lambda i: (0, 0)),
        ],
        out_specs=[pl.BlockSpec((4, KV_TM, HEAD_DIM), lambda i: (0, i, 0)), tile(), tile(), tile(), tile()],
        out_shape=[jax.ShapeDtypeStruct((4, n, HEAD_DIM), F32)]
        + [jax.ShapeDtypeStruct((n, LANES), BF16)] * 4,
        compiler_params=_cparams(("parallel",)),
        name="nsa_kv",
    )(x, w)


CMP_HALF = CMP_STRIDE * HEAD_DIM


def _compress_kernel(z_ref, pos_ref, w1_ref, b1_ref, w2_ref, o_ref):
    r = z_ref[0]
    rows = r.shape[0]
    lo = _dot((r + pos_ref[0, 0:1, :]).astype(BF16), w1_ref[0, :CMP_HALF, :])
    hi = _dot((r + pos_ref[0, 1:2, :]).astype(BF16), w1_ref[0, CMP_HALF:, :])
    pre = lo + pltpu.roll(hi, rows - 1, 0) + b1_ref[0]
    o_ref[0] = _dot(jax.nn.gelu(pre).astype(BF16), w2_ref[0]).astype(BF16)


def _compress(z, pos, w1, b1, w2, bsz):
    nb, rows, _ = z.shape
    per_kv = NSA_GROUPS * bsz
    return pl.pallas_call(
        _compress_kernel,
        grid=(nb,),
        in_specs=[
            pl.BlockSpec((1, rows, CMP_HALF), lambda i: (i, 0, 0)),
            pl.BlockSpec((1, 2, CMP_HALF), lambda i: (i // per_kv, 0, 0)),
            pl.BlockSpec((1, 2 * CMP_HALF, CMP_HIDDEN), lambda i: (i // per_kv, 0, 0)),
            pl.BlockSpec((1, 1, CMP_HIDDEN), lambda i: (i // per_kv, 0, 0)),
            pl.BlockSpec((1, CMP_HIDDEN, HEAD_DIM), lambda i: (i // per_kv, 0, 0)),
        ],
        out_specs=pl.BlockSpec((1, rows, HEAD_DIM), lambda i: (i, 0, 0)),
        out_shape=jax.ShapeDtypeStruct((nb, rows, HEAD_DIM), BF16),
        compiler_params=_cparams(("parallel",)),
        name="nsa_compress",
    )(z, pos, w1, b1, w2)


NSA_TM = 512
NSA_GATE_COLS = NSA_GROUPS * LANES
NSA_IN_COLS = HQ + NSA_GATE_COLS + MEM_WIDTH


def _nsa_proj_kernel(x_ref, w_ref, q_ref, gt_ref, qm_ref):
    y = _dot(x_ref[...].astype(BF16), w_ref[...])
    q_ref[...] = y[:, :HQ].astype(BF16)
    gt_ref[...] = jax.nn.sigmoid(y[:, HQ:HQ + NSA_GATE_COLS])
    qm_ref[...] = y[:, HQ + NSA_GATE_COLS:].astype(BF16)


def _nsa_proj(x, w):
    n = x.shape[0]
    return pl.pallas_call(
        _nsa_proj_kernel,
        grid=(n // NSA_TM,),
        in_specs=[
            pl.BlockSpec((NSA_TM, D_MODEL), lambda i: (i, 0)),
            pl.BlockSpec((D_MODEL, NSA_IN_COLS), lambda i: (0, 0)),
        ],
        out_specs=[
            pl.BlockSpec((NSA_TM, HQ), lambda i: (i, 0)),
            pl.BlockSpec((NSA_TM, NSA_GATE_COLS), lambda i: (i, 0)),
            pl.BlockSpec((NSA_TM, MEM_WIDTH), lambda i: (i, 0)),
        ],
        out_shape=[
            jax.ShapeDtypeStruct((n, HQ), BF16),
            jax.ShapeDtypeStruct((n, NSA_GATE_COLS), F32),
            jax.ShapeDtypeStruct((n, MEM_WIDTH), BF16),
        ],
        compiler_params=_cparams(("parallel",)),
        name="nsa_proj",
    )(x, w)


NSA_TQ = 256
NSA_TK = 512
NSA_ROWS = NSA_HPG * NSA_TQ


def _nsa_attn_kernel(slopes_ref, q_ref, gt_ref, kc_ref, vc_ref, agg_ref, ks_ref, vs_ref,
                     kwp_ref, kwc_ref, vwp_ref, vwc_ref, o_ref,
                     qs_ref, sel_ref, m_ref, l_ref, acc_ref, oc_ref, ow_ref, *, n_cmp):
    g = pl.program_id(1)
    i = pl.program_id(2)
    j = pl.program_id(3)
    t0 = i * NSA_TQ
    jmax = (t0 + NSA_TQ - 1) // NSA_TK
    hrows = lambda hh: slice(hh * NSA_TQ, (hh + 1) * NSA_TQ)

    @pl.when(j == 0)
    def _():
        for hh in range(NSA_HPG):
            qs_ref[hrows(hh), :] = q_ref[:, hh * LANES:(hh + 1) * LANES]
        qs = qs_ref[...]
        m_ref[...] = jnp.full(m_ref.shape, NEG, F32)
        l_ref[...] = jnp.zeros(l_ref.shape, F32)
        acc_ref[...] = jnp.zeros(acc_ref.shape, F32)

        nc = kc_ref.shape[1]
        tq_c = t0 + lax.broadcasted_iota(jnp.int32, (NSA_TQ, nc), 0)
        blk_c = lax.broadcasted_iota(jnp.int32, (NSA_TQ, nc), 1)
        keep_c = (blk_c * CMP_STRIDE + (CMP_LEN - 1) <= tq_c) & (blk_c < n_cmp)
        dist_c = tq_c.astype(F32) - (blk_c.astype(F32) * CMP_STRIDE + (CMP_LEN - 1) * 0.5)
        s_all = _dot_nt(qs, kc_ref[0])
        imp = jnp.zeros((NSA_TQ, LANES), F32)
        p_list = []
        for hh in range(NSA_HPG):
            s = s_all[hrows(hh)] - slopes_ref[g * NSA_HPG + hh] * dist_c
            s = jnp.where(keep_c, s, NEG)
            mx = jnp.max(s, -1, keepdims=True)
            e = jnp.where(keep_c, jnp.exp(s - mx), 0.0)
            p = (e / jnp.maximum(jnp.sum(e, -1, keepdims=True), 1e-30)).astype(BF16)
            imp = imp + _dot(p, agg_ref[...])
            p_list.append(p)
        oc_ref[...] = _dot(jnp.concatenate(p_list, 0), vc_ref[0])

        tq_s = t0 + lax.broadcasted_iota(jnp.int32, (NSA_TQ, LANES), 0)
        blk = lax.broadcasted_iota(jnp.int32, (NSA_TQ, LANES), 1)
        cur = tq_s >> SEL_SHIFT
        forced = (blk == 0) | (blk == cur) | (blk == cur - 1)
        imp = jnp.where(forced, imp + FORCE_BONUS, imp)
        imp = jnp.where(blk * SEL_LEN <= tq_s, imp, NEG)
        blk_f = blk.astype(F32)
        sel = jnp.zeros((NSA_TQ, LANES), F32)
        for _ in range(SEL_TOPK):
            mx = jnp.max(imp, -1, keepdims=True)
            first = jnp.min(jnp.where(imp == mx, blk_f, float(LANES)), -1, keepdims=True)
            hit = blk_f == first
            sel = jnp.where(hit, 1.0, sel)
            imp = jnp.where(hit, -jnp.inf, imp)
        sel_ref[...] = sel.astype(BF16)

        wt = t0 // NSA_TK
        kw = jnp.concatenate([kwp_ref[...], kwc_ref[...]], 0)
        vw = jnp.concatenate([vwp_ref[...], vwc_ref[...]], 0)
        tq_w = t0 + lax.broadcasted_iota(jnp.int32, (NSA_TQ, 2 * NSA_TK), 0)
        pos_w = (wt - 1) * NSA_TK + lax.broadcasted_iota(jnp.int32, (NSA_TQ, 2 * NSA_TK), 1)
        dist_w = tq_w - pos_w
        keep_w = (dist_w >= 0) & (dist_w < WINDOW) & (pos_w >= 0)
        dist_wf = dist_w.astype(F32)
        s_all = _dot_nt(qs, kw)
        p_list = []
        for hh in range(NSA_HPG):
            s = s_all[hrows(hh)] - slopes_ref[g * NSA_HPG + hh] * dist_wf
            s = jnp.where(keep_w, s, NEG)
            mx = jnp.max(s, -1, keepdims=True)
            e = jnp.where(keep_w, jnp.exp(s - mx), 0.0)
            p_list.append((e / jnp.maximum(jnp.sum(e, -1, keepdims=True), 1e-30)).astype(BF16))
        ow_ref[...] = _dot(jnp.concatenate(p_list, 0), vw)

    def sel_tile(diag):
        qs = qs_ref[...]
        s_all = _dot_nt(qs, ks_ref[...])
        pos_e = j * NSA_TK + lax.broadcasted_iota(jnp.int32, (LANES, NSA_TK), 1)
        blk_e = lax.broadcasted_iota(jnp.int32, (LANES, NSA_TK), 0)
        expand = jnp.where((pos_e >> SEL_SHIFT) == blk_e, 1.0, 0.0).astype(BF16)
        keep = _dot(sel_ref[...], expand) > 0.5
        tq_k = t0 + lax.broadcasted_iota(jnp.int32, (NSA_TQ, NSA_TK), 0)
        pos_k = j * NSA_TK + lax.broadcasted_iota(jnp.int32, (NSA_TQ, NSA_TK), 1)
        dist = tq_k - pos_k
        if diag:
            keep = keep & (dist >= 0)
        dist_f = dist.astype(F32)
        p_list = []
        for hh in range(NSA_HPG):
            r = hrows(hh)
            s = s_all[r] - slopes_ref[g * NSA_HPG + hh] * dist_f
            s = jnp.where(keep, s, NEG)
            m_old = m_ref[r]
            m_new = jnp.maximum(m_old, jnp.max(s, -1, keepdims=True))
            alpha = jnp.exp(m_old - m_new)
            p = jnp.where(keep, jnp.exp(s - m_new), 0.0)
            l_ref[r] = alpha * l_ref[r] + jnp.sum(p, -1, keepdims=True)
            acc_ref[r] = alpha * acc_ref[r]
            m_ref[r] = m_new
            p_list.append(p.astype(BF16))
        acc_ref[...] += _dot(jnp.concatenate(p_list, 0), vs_ref[...])

    @pl.when(j < jmax)
    def _():
        sel_tile(False)

    @pl.when(j == jmax)
    def _():
        sel_tile(True)
        gt = gt_ref[...]
        for hh in range(NSA_HPG):
            r = hrows(hh)
            o_s = acc_ref[r] / jnp.maximum(l_ref[r], 1e-30)
            c = hh * NSA_BRANCHES
            o = gt[:, c:c + 1] * oc_ref[r] + gt[:, c + 1:c + 2] * o_s + gt[:, c + 2:c + 3] * ow_ref[r]
            o_ref[:, hh * LANES:(hh + 1) * LANES] = o.astype(BF16)


def _nsa_attn(slopes, q, gates, kc, vc, agg, ks, vs, kw, vw, bsz, seq):
    nq = seq // NSA_TQ
    nk = seq // NSA_TK
    nc = kc.shape[1]
    n_cmp = (seq - CMP_LEN) // CMP_STRIDE + 1
    gw = NSA_HPG * LANES

    def jclamp(i, j):
        return jnp.minimum(j, (i * NSA_TQ + NSA_TQ - 1) // NSA_TK)

    def wcur(i):
        return (i * NSA_TQ) // NSA_TK

    sel_tile = lambda b, g, i, j: (b * nk + jclamp(i, j), 0)
    win_prev = lambda b, g, i, j: (b * nk + jnp.maximum(wcur(i) - 1, 0), 0)
    win_cur = lambda b, g, i, j: (b * nk + wcur(i), 0)
    return pl.pallas_call(
        functools.partial(_nsa_attn_kernel, n_cmp=n_cmp),
        grid=(bsz, NSA_GROUPS, nq, nk),
        in_specs=[
            pl.BlockSpec(memory_space=pltpu.SMEM),
            pl.BlockSpec((NSA_TQ, gw), lambda b, g, i, j: (b * nq + i, g)),
            pl.BlockSpec((NSA_TQ, LANES), lambda b, g, i, j: (b * nq + i, g)),
            pl.BlockSpec((1, nc, LANES), lambda b, g, i, j: (b, 0, 0)),
            pl.BlockSpec((1, nc, LANES), lambda b, g, i, j: (b, 0, 0)),
            pl.BlockSpec((nc, LANES), lambda b, g, i, j: (0, 0)),
            pl.BlockSpec((NSA_TK, LANES), sel_tile),
            pl.BlockSpec((NSA_TK, LANES), sel_tile),
            pl.BlockSpec((NSA_TK, LANES), win_prev),
            pl.BlockSpec((NSA_TK, LANES), win_cur),
            pl.BlockSpec((NSA_TK, LANES), win_prev),
            pl.BlockSpec((NSA_TK, LANES), win_cur),
        ],
        out_specs=pl.BlockSpec((NSA_TQ, gw), lambda b, g, i, j: (b * nq + i, g)),
        out_shape=jax.ShapeDtypeStruct((bsz * seq, HQ), BF16),
        scratch_shapes=[
            pltpu.VMEM((NSA_ROWS, LANES), BF16),
            pltpu.VMEM((NSA_TQ, LANES), BF16),
            pltpu.VMEM((NSA_ROWS, 1), F32),
            pltpu.VMEM((NSA_ROWS, 1), F32),
            pltpu.VMEM((NSA_ROWS, LANES), F32),
            pltpu.VMEM((NSA_ROWS, LANES), F32),
            pltpu.VMEM((NSA_ROWS, LANES), F32),
        ],
        compiler_params=_cparams(("parallel", "parallel", "parallel", "arbitrary")),
        name="nsa_attn",
    )(slopes, q, gates, kc, vc, agg, ks, vs, kw, kw, vw, vw)


def _alibi_slopes(n):
    def pow2(k):
        start = 2.0 ** (-8.0 / k)
        return [start ** (i + 1) for i in range(k)]
    if math.log2(n).is_integer():
        s = pow2(n)
    else:
        c = 2 ** math.floor(math.log2(n))
        s = pow2(c) + pow2(2 * c)[0::2][: n - c]
    return np.asarray(s, np.float32)


def _head_slots(w, width):
    r = w.shape[0]
    w3 = w.reshape(r, TOK_HEADS, width)
    return jnp.pad(w3, ((0, 0), (0, 0), (0, LANES - width))).reshape(r, HQ)


def _rope_pair(w_x1, w_x2):
    half = MLA_ROPE // 2
    r = w_x1.shape[0]
    z_lo = jnp.zeros((r, MLA_NOPE), w_x1.dtype)
    z_hi = jnp.zeros((r, LANES - MLA_NOPE - MLA_ROPE), w_x1.dtype)
    assert w_x1.shape[1] == half
    return (jnp.concatenate([z_lo, w_x1, w_x2, z_hi], 1),
            jnp.concatenate([z_lo, -w_x2, w_x1, z_hi], 1))


def _mla_weights(w_in, w_uq, w_ukv):
    half = MLA_ROPE // 2
    o = MLA_Q_RANK + MLA_KV_RANK
    kr, kr_sw = _rope_pair(w_in[:, o:o + half], w_in[:, o + half:o + MLA_ROPE])
    w_in_all = jnp.concatenate([w_in[:, :o], kr, kr_sw, w_in[:, o + MLA_ROPE:]], 1)
    wq3 = w_uq.reshape(MLA_Q_RANK, TOK_HEADS, MLA_NOPE + MLA_ROPE)
    nope, x1, x2 = wq3[..., :MLA_NOPE], wq3[..., MLA_NOPE:MLA_NOPE + half], wq3[..., MLA_NOPE + half:]
    z = jnp.zeros((MLA_Q_RANK, TOK_HEADS, LANES - MLA_NOPE - MLA_ROPE), w_uq.dtype)
    wq = jnp.concatenate([nope, x1, x2, z], -1).reshape(MLA_Q_RANK, HQ)
    wq_sw = jnp.concatenate([jnp.zeros_like(nope), -x2, x1, z], -1).reshape(MLA_Q_RANK, HQ)
    wkv3 = w_ukv.reshape(MLA_KV_RANK, TOK_HEADS, MLA_NOPE + HEAD_DIM)
    wk = _head_slots(wkv3[..., :MLA_NOPE].reshape(MLA_KV_RANK, -1), MLA_NOPE)
    wv = wkv3[..., MLA_NOPE:].reshape(MLA_KV_RANK, TOK_WIDTH)
    return (w_in_all.astype(BF16), jnp.concatenate([wq, wq_sw], 1).astype(BF16),
            wk.astype(BF16), wv.astype(BF16))


def _rope_tables(seq):
    half = MLA_ROPE // 2
    freq = ROPE_THETA ** (-jnp.arange(half, dtype=F32) / half)
    ang = jnp.arange(seq).astype(F32)[:, None] * freq[None, :]
    cos, sin = jnp.cos(ang), jnp.sin(ang)
    ones = jnp.ones((seq, MLA_NOPE), F32)
    z_hi = jnp.zeros((seq, LANES - MLA_NOPE - MLA_ROPE), F32)
    cos_t = jnp.concatenate([ones, cos, cos, z_hi], 1)
    sin_t = jnp.concatenate([jnp.zeros_like(ones), sin, sin, z_hi], 1)
    return cos_t, sin_t


def _group_slots(w):
    r = w.shape[0]
    w4 = w.reshape(r, NSA_GROUPS, NSA_HPG, HEAD_DIM)
    tiles = []
    for g in range(NSA_GROUPS):
        pad = ((0, 0), (0, 0), (g * HEAD_DIM, LANES - (g + 1) * HEAD_DIM))
        tiles.append(jnp.pad(w4[:, g], pad))
    return jnp.stack(tiles, 1).reshape(r, HQ)


def _nsa_in_weights(w_in):
    scale = HEAD_DIM ** -0.5
    wq = _group_slots(w_in[:, :TOK_WIDTH] * scale)
    ng = NSA_HPG * NSA_BRANCHES
    wg = w_in[:, TOK_WIDTH:TOK_WIDTH + TOK_HEADS * NSA_BRANCHES].reshape(-1, NSA_GROUPS, ng)
    wg = jnp.pad(wg, ((0, 0), (0, 0), (0, LANES - ng))).reshape(-1, NSA_GATE_COLS)
    return jnp.concatenate([wq, wg, w_in[:, TOK_WIDTH + TOK_HEADS * NSA_BRANCHES:]], 1).astype(BF16)


def _nsa_kv_weights(w_kv):
    w5 = w_kv.reshape(D_MODEL, NSA_BRANCHES, 2, NSA_GROUPS * HEAD_DIM)
    cmp_cols = []
    for kv in range(2):
        for g in range(NSA_GROUPS):
            c = w5[:, 0, kv, g * HEAD_DIM:(g + 1) * HEAD_DIM]
            cmp_cols.append(jnp.pad(c, ((0, 0), (0, LANES - HEAD_DIM))))
    rest = [w5[:, 1, 0], w5[:, 1, 1], w5[:, 2, 0], w5[:, 2, 1]]
    return jnp.concatenate(cmp_cols + rest, 1).astype(BF16)


def _out_weights_nsa(w_tok):
    return _group_slots(w_tok.T).T


def _agg_matrix(seq, rows):
    n_cmp = (seq - CMP_LEN) // CMP_STRIDE + 1
    n_sel = seq // SEL_LEN
    cmp_start = np.arange(n_cmp) * CMP_STRIDE
    sel_start = np.arange(n_sel) * SEL_LEN
    overlap = np.clip(np.minimum(cmp_start[:, None] + CMP_LEN, sel_start[None, :] + SEL_LEN)
                      - np.maximum(cmp_start[:, None], sel_start[None, :]), 0, None)
    agg = np.zeros((rows, LANES), np.float32)
    agg[:n_cmp, :n_sel] = overlap / CMP_LEN
    return jnp.asarray(agg, BF16)


def kernel(x, mem, ln_g, ln_b, ffn_w_gu, ffn_w_down, w_mem_kv, w_out, mla_w_in, mla_q_norm_g, mla_kv_norm_g,
           mla_w_uq, mla_w_ukv, nsa_w_in, nsa_w_kv, cmp_pos, cmp_w1, cmp_b1, cmp_w2):
    bsz, seq, _ = x.shape
    n = bsz * seq
    assert seq % NSA_TK == 0 and seq % MLA_TQ == 0 and seq // SEL_LEN <= LANES
    h = x.reshape(n, D_MODEL)
    cos_t, sin_t = _rope_tables(seq)
    slopes = jnp.asarray(_alibi_slopes(TOK_HEADS))
    rows_c = seq // CMP_STRIDE
    agg = _agg_matrix(seq, rows_c)
    ln = lambda layer, k: (ln_g[layer, k][None, :], ln_b[layer, k][None, :])
    shared = None

    for layer in range(DEPTH):
        h = _ffn_ln(h, ffn_w_gu[layer, 0].astype(BF16), ffn_w_down[layer, 0].astype(BF16), *ln(layer, 0))
        km, vm = _mem_kv(mem, w_mem_kv[layer].astype(BF16))
        w_tok = w_out[layer, :TOK_WIDTH]
        w_memo = w_out[layer, TOK_WIDTH:].astype(BF16)
        if layer < N_A_LAYERS:
            w_in_all, wq, wk, wv = _mla_weights(mla_w_in[layer], mla_w_uq[layer], mla_w_ukv[layer])
            q, k, v, q_mem = _mla_proj(h, w_in_all, mla_q_norm_g[layer][None, :], mla_kv_norm_g[layer][None, :],
                                       wq, wk, wv, cos_t, sin_t, seq)
            o_tok = _mla_attn(q, k, v, bsz, seq)
            w_tok = w_tok.astype(BF16)
        else:
            kc, vc, ks, vs, kw, vw = shared
            q, gates, q_mem = _nsa_proj(h, _nsa_in_weights(nsa_w_in[layer - N_A_LAYERS]))
            o_tok = _nsa_attn(slopes, q, gates, kc, vc, agg, ks, vs, kw, vw, bsz, seq)
            w_tok = _out_weights_nsa(w_tok).astype(BF16)
        h = _mix_out(h, o_tok, q_mem, km, vm, w_tok, w_memo, *ln(layer, 1), seq)
        h = _ffn_ln(h, ffn_w_gu[layer, 1].astype(BF16), ffn_w_down[layer, 1].astype(BF16), *ln(layer, 2))
        if layer == N_A_LAYERS - 1:
            zc, ks, vs, kw, vw = _nsa_kv(h, _nsa_kv_weights(nsa_w_kv))
            z = zc.reshape(4 * bsz, rows_c, CMP_HALF)
            c = _compress(z, cmp_pos.reshape(2, 2, CMP_HALF), cmp_w1.astype(BF16), cmp_b1[:, None, :],
                          cmp_w2.astype(BF16), bsz)
            c = c.reshape(2, NSA_GROUPS, bsz, rows_c, HEAD_DIM)
            kc = jnp.concatenate([c[0, g] for g in range(NSA_GROUPS)], -1)
            vc = jnp.concatenate([c[1, g] for g in range(NSA_GROUPS)], -1)
            shared = (kc, vc, ks, vs, kw, vw)
    return h.reshape(bsz, seq, D_MODEL)
```

```python
import functools
import math

import numpy as np
import jax
import jax.numpy as jnp
from jax import lax
from jax.experimental import pallas as pl
from jax.experimental.pallas import tpu as pltpu

F32 = jnp.float32
BF16 = jnp.bfloat16

D_MODEL = 1024
DEPTH = 4
N_A_LAYERS = DEPTH // 2

TOK_HEADS = 12
HEAD_DIM = 64
MEM_HEADS = 4
MEM_HEAD_DIM = 64
TOK_WIDTH = TOK_HEADS * HEAD_DIM
MEM_WIDTH = MEM_HEADS * MEM_HEAD_DIM

MLA_Q_RANK = 256
MLA_KV_RANK = 128
MLA_NOPE = 64
MLA_ROPE = 32
ROPE_THETA = 10000.0

NSA_GROUPS = 2
NSA_HPG = TOK_HEADS // NSA_GROUPS
NSA_BRANCHES = 3
CMP_LEN = 32
CMP_STRIDE = 16
CMP_HIDDEN = 256
SEL_LEN = 64
SEL_SHIFT = 6
SEL_TOPK = 16
WINDOW = 512

D_FF = 2816
DN_ALPHA = (2 * DEPTH) ** 0.25
LN_EPS = 1e-5
RMS_EPS = 1e-6
NEG = -1e30
FORCE_BONUS = 1e4

LANES = 128
VMEM_LIMIT = 56 * 1024 * 1024


def _cparams(sem):
    return pltpu.CompilerParams(dimension_semantics=sem, vmem_limit_bytes=VMEM_LIMIT)


def _layer_norm(y, g, b):
    mu = jnp.mean(y, -1, keepdims=True)
    yc = y - mu
    var = jnp.mean(yc * yc, -1, keepdims=True)
    return yc * lax.rsqrt(var + LN_EPS) * g + b


def _rms_norm(x, g):
    return x * lax.rsqrt(jnp.mean(x * x, -1, keepdims=True) + RMS_EPS) * g


def _dot(a, b):
    return jnp.dot(a, b, preferred_element_type=F32)


def _dot_nt(a, b):
    return lax.dot_general(a, b, (((1,), (1,)), ((), ())), preferred_element_type=F32)


FFN_TM = 512
FFN_TF = 1408


def _ffn_kernel(x_ref, wg_ref, wu_ref, wd_ref, g_ref, b_ref, o_ref, acc_ref, *, nj):
    j = pl.program_id(1)
    xb = x_ref[...].astype(BF16)
    gate = _dot(xb, wg_ref[...])
    up = _dot(xb, wu_ref[...])
    h = (gate * jax.nn.sigmoid(gate) * up).astype(BF16)
    part = _dot(h, wd_ref[...])

    @pl.when(j == 0)
    def _():
        acc_ref[...] = part

    @pl.when(j > 0)
    def _():
        acc_ref[...] += part

    @pl.when(j == nj - 1)
    def _():
        y = DN_ALPHA * x_ref[...] + 0.5 * acc_ref[...]
        o_ref[...] = _layer_norm(y, g_ref[...], b_ref[...])


def _ffn_ln(x, w_gu, w_down, g, b):
    n = x.shape[0]
    nj = D_FF // FFN_TF
    return pl.pallas_call(
        functools.partial(_ffn_kernel, nj=nj),
        grid=(n // FFN_TM, nj),
        in_specs=[
            pl.BlockSpec((FFN_TM, D_MODEL), lambda i, j: (i, 0)),
            pl.BlockSpec((D_MODEL, FFN_TF), lambda i, j: (0, j)),
            pl.BlockSpec((D_MODEL, FFN_TF), lambda i, j: (0, j + nj)),
            pl.BlockSpec((FFN_TF, D_MODEL), lambda i, j: (j, 0)),
            pl.BlockSpec((1, D_MODEL), lambda i, j: (0, 0)),
            pl.BlockSpec((1, D_MODEL), lambda i, j: (0, 0)),
        ],
        out_specs=pl.BlockSpec((FFN_TM, D_MODEL), lambda i, j: (i, 0)),
        out_shape=jax.ShapeDtypeStruct((n, D_MODEL), F32),
        scratch_shapes=[pltpu.VMEM((FFN_TM, D_MODEL), F32)],
        compiler_params=_cparams(("parallel", "arbitrary")),
        name="ffn_ln",
    )(x, w_gu, w_gu, w_down, g, b)


def _memkv_kernel(mem_ref, w_ref, km_ref, v_ref):
    kv = _dot(mem_ref[0].astype(BF16), w_ref[...])
    k = kv[:, :MEM_WIDTH]
    v_ref[0] = kv[:, MEM_WIDTH:].astype(BF16)
    lane = lax.broadcasted_iota(jnp.int32, k.shape, 1)
    for h in range(MEM_HEADS):
        in_head = (lane >= h * MEM_HEAD_DIM) & (lane < (h + 1) * MEM_HEAD_DIM)
        km_ref[0, h] = jnp.where(in_head, k, 0.0).astype(BF16)


def _mem_kv(mem, w_mem_kv):
    bsz, m, _ = mem.shape
    return pl.pallas_call(
        _memkv_kernel,
        grid=(bsz,),
        in_specs=[
            pl.BlockSpec((1, m, D_MODEL), lambda b: (b, 0, 0)),
            pl.BlockSpec((D_MODEL, 2 * MEM_WIDTH), lambda b: (0, 0)),
        ],
        out_specs=[
            pl.BlockSpec((1, MEM_HEADS, m, MEM_WIDTH), lambda b: (b, 0, 0, 0)),
            pl.BlockSpec((1, m, MEM_WIDTH), lambda b: (b, 0, 0)),
        ],
        out_shape=[
            jax.ShapeDtypeStruct((bsz, MEM_HEADS, m, MEM_WIDTH), BF16),
            jax.ShapeDtypeStruct((bsz, m, MEM_WIDTH), BF16),
        ],
        compiler_params=_cparams(("parallel",)),
        name="mem_kv",
    )(mem, w_mem_kv)


OUT_TM = 512


def _mix_out_kernel(x_ref, ot_ref, qm_ref, km_ref, vm_ref, wt_ref, wm_ref, g_ref, b_ref, o_ref):
    qm = qm_ref[...]
    vm = vm_ref[0]
    lane = lax.broadcasted_iota(jnp.int32, (qm.shape[0], MEM_WIDTH), 1)
    o_mem = jnp.zeros((qm.shape[0], MEM_WIDTH), F32)
    for h in range(MEM_HEADS):
        s = _dot_nt(qm, km_ref[0, h]) * (MEM_HEAD_DIM ** -0.5)
        m = jnp.max(s, -1, keepdims=True)
        e = jnp.exp(s - m)
        p = (e / jnp.sum(e, -1, keepdims=True)).astype(BF16)
        pv = _dot(p, vm)
        in_head = (lane >= h * MEM_HEAD_DIM) & (lane < (h + 1) * MEM_HEAD_DIM)
        o_mem = jnp.where(in_head, pv, o_mem)
    mix = _dot(ot_ref[...], wt_ref[...]) + _dot(o_mem.astype(BF16), wm_ref[...])
    y = DN_ALPHA * x_ref[...] + mix
    o_ref[...] = _layer_norm(y, g_ref[...], b_ref[...])


def _mix_out(x, o_tok, q_mem, km, vm, w_tok, w_mem, g, b, seq):
    n = x.shape[0]
    kt = o_tok.shape[1]
    m = vm.shape[1]
    per_b = seq // OUT_TM
    return pl.pallas_call(
        _mix_out_kernel,
        grid=(n // OUT_TM,),
        in_specs=[
            pl.BlockSpec((OUT_TM, D_MODEL), lambda i: (i, 0)),
            pl.BlockSpec((OUT_TM, kt), lambda i: (i, 0)),
            pl.BlockSpec((OUT_TM, MEM_WIDTH), lambda i: (i, 0)),
            pl.BlockSpec((1, MEM_HEADS, m, MEM_WIDTH), lambda i: (i // per_b, 0, 0, 0)),
            pl.BlockSpec((1, m, MEM_WIDTH), lambda i: (i // per_b, 0, 0)),
            pl.BlockSpec((kt, D_MODEL), lambda i: (0, 0)),
            pl.BlockSpec((MEM_WIDTH, D_MODEL), lambda i: (0, 0)),
            pl.BlockSpec((1, D_MODEL), lambda i: (0, 0)),
            pl.BlockSpec((1, D_MODEL), lambda i: (0, 0)),
        ],
        out_specs=pl.BlockSpec((OUT_TM, D_MODEL), lambda i: (i, 0)),
        out_shape=jax.ShapeDtypeStruct((n, D_MODEL), F32),
        compiler_params=_cparams(("parallel",)),
        name="mix_out",
    )(x, o_tok, q_mem, km, vm, w_tok, w_mem, g, b)


MLA_TM = 512
MLA_IN_COLS = MLA_Q_RANK + MLA_KV_RANK + 2 * LANES + MEM_WIDTH
HQ = TOK_HEADS * LANES


def _mla_proj_kernel(x_ref, win_ref, qg_ref, kvg_ref, wq_ref, wk_ref, wvt_ref, cos_ref, sin_ref,
                     q_ref, k_ref, vt_ref, qm_ref):
    xb = x_ref[...].astype(BF16)
    hh = _dot(xb, win_ref[...])
    c_q = hh[:, :MLA_Q_RANK]
    c_kv = hh[:, MLA_Q_RANK:MLA_Q_RANK + MLA_KV_RANK]
    o = MLA_Q_RANK + MLA_KV_RANK
    kr = hh[:, o:o + LANES]
    kr_sw = hh[:, o + LANES:o + 2 * LANES]
    qm_ref[...] = hh[:, o + 2 * LANES:].astype(BF16)
    cos = cos_ref[...]
    sin = sin_ref[...]
    qq = _dot(_rms_norm(c_q, qg_ref[...]).astype(BF16), wq_ref[...])
    ckv = _rms_norm(c_kv, kvg_ref[...]).astype(BF16)
    kk = _dot(ckv, wk_ref[...])
    vt_ref[0] = _dot_nt(wvt_ref[...], ckv).astype(BF16)
    kr_rot = kr * cos + kr_sw * sin
    for h in range(TOK_HEADS):
        sl = slice(h * LANES, (h + 1) * LANES)
        sl2 = slice(HQ + h * LANES, HQ + (h + 1) * LANES)
        q_ref[:, sl] = (qq[:, sl] * cos + qq[:, sl2] * sin).astype(BF16)
        k_ref[:, sl] = (kk[:, sl] + kr_rot).astype(BF16)


def _mla_proj(x, w_in, qg, kvg, w_q, w_k, w_vt, cos_t, sin_t, seq):
    n = x.shape[0]
    per_b = seq // MLA_TM
    full = lambda shape: pl.BlockSpec(shape, lambda i: (0, 0))
    return pl.pallas_call(
        _mla_proj_kernel,
        grid=(n // MLA_TM,),
        in_specs=[
            pl.BlockSpec((MLA_TM, D_MODEL), lambda i: (i, 0)),
            full((D_MODEL, MLA_IN_COLS)),
            full((1, MLA_Q_RANK)),
            full((1, MLA_KV_RANK)),
            full((MLA_Q_RANK, 2 * HQ)),
            full((MLA_KV_RANK, HQ)),
            full((TOK_WIDTH, MLA_KV_RANK)),
            pl.BlockSpec((MLA_TM, LANES), lambda i: (i % per_b, 0)),
            pl.BlockSpec((MLA_TM, LANES), lambda i: (i % per_b, 0)),
        ],
        out_specs=[
            pl.BlockSpec((MLA_TM, HQ), lambda i: (i, 0)),
            pl.BlockSpec((MLA_TM, HQ), lambda i: (i, 0)),
            pl.BlockSpec((1, TOK_WIDTH, MLA_TM), lambda i: (i // per_b, 0, i % per_b)),
            pl.BlockSpec((MLA_TM, MEM_WIDTH), lambda i: (i, 0)),
        ],
        out_shape=[
            jax.ShapeDtypeStruct((n, HQ), BF16),
            jax.ShapeDtypeStruct((n, HQ), BF16),
            jax.ShapeDtypeStruct((n // seq, TOK_WIDTH, seq), BF16),
            jax.ShapeDtypeStruct((n, MEM_WIDTH), BF16),
        ],
        compiler_params=_cparams(("parallel",)),
        name="mla_proj",
    )(x, w_in, qg, kvg, w_q, w_k, w_vt, cos_t, sin_t)


MLA_TQ = 512
MLA_TK = 512
MLA_SCALE = (MLA_NOPE + MLA_ROPE) ** -0.5


def _mla_attn_kernel(q_ref, k_ref, vt_ref, o_ref, m_ref, l_ref, acc_ref):
    i = pl.program_id(2)
    m_ref[...] = jnp.full(m_ref.shape, NEG, F32)
    l_ref[...] = jnp.zeros(l_ref.shape, F32)
    acc_ref[...] = jnp.zeros(acc_ref.shape, F32)

    def tile(j, masked):
        start = pl.multiple_of(j * MLA_TK, MLA_TK)
        kt = k_ref[pl.ds(start, MLA_TK), :]
        vt = vt_ref[0, :, pl.ds(start, MLA_TK)]
        if masked:
            krow = lax.broadcasted_iota(jnp.int32, (MLA_TK, MLA_TQ), 0)
            qcol = lax.broadcasted_iota(jnp.int32, (MLA_TK, MLA_TQ), 1)
            keep = krow <= qcol
        for hd in range(2):
            sl = slice(hd * LANES, (hd + 1) * LANES)
            rows = slice(hd * HEAD_DIM, (hd + 1) * HEAD_DIM)
            s = _dot_nt(kt[:, sl], q_ref[:, sl])
            if masked:
                s = jnp.where(keep, s, NEG)
            m_old = m_ref[hd]
            m_new = jnp.maximum(m_old, jnp.max(s, 0, keepdims=True))
            alpha = jnp.exp((m_old - m_new) * MLA_SCALE)
            p = jnp.exp((s - m_new) * MLA_SCALE)
            if masked:
                p = jnp.where(keep, p, 0.0)
            l_ref[hd] = alpha * l_ref[hd] + jnp.sum(p, 0, keepdims=True)
            acc_ref[rows, :] = alpha * acc_ref[rows, :] + _dot(vt[rows, :], p.astype(BF16))
            m_ref[hd] = m_new

    def body(j, carry):
        tile(j, False)
        return carry

    lax.fori_loop(0, i, body, 0)
    tile(i, True)
    inv = jnp.concatenate(
        [jnp.broadcast_to(1.0 / jnp.maximum(l_ref[hd], 1e-30), (HEAD_DIM, MLA_TQ)) for hd in range(2)], 0)
    o_ref[...] = (acc_ref[...] * inv).T.astype(BF16)


def _mla_attn(q, k, vt, bsz, seq):
    assert MLA_TQ == MLA_TK
    nq = seq // MLA_TQ
    return pl.pallas_call(
        _mla_attn_kernel,
        grid=(bsz, TOK_HEADS // 2, nq),
        in_specs=[
            pl.BlockSpec((MLA_TQ, 2 * LANES), lambda b, p, i: (b * nq + i, p)),
            pl.BlockSpec((seq, 2 * LANES), lambda b, p, i: (b, p)),
            pl.BlockSpec((1, 2 * HEAD_DIM, seq), lambda b, p, i: (b, p, 0)),
        ],
        out_specs=pl.BlockSpec((MLA_TQ, LANES), lambda b, p, i: (b * nq + i, p)),
        out_shape=jax.ShapeDtypeStruct((bsz * seq, TOK_WIDTH), BF16),
        scratch_shapes=[
            pltpu.VMEM((2, 1, MLA_TQ), F32),
            pltpu.VMEM((2, 1, MLA_TQ), F32),
            pltpu.VMEM((2 * HEAD_DIM, MLA_TQ), F32),
        ],
        compiler_params=_cparams(("parallel", "parallel", "arbitrary")),
        name="mla_attn",
    )(q, k, vt)


KV_TM = 512
NSA_CMP_COLS = 4 * LANES
NSA_KV_COLS = NSA_CMP_COLS + 2 * LANES


def _nsa_kv_kernel(x_ref, w_ref, wvt_ref, zc_ref, ks_ref, kw_ref, vst_ref, vwt_ref):
    xb = x_ref[...].astype(BF16)
    y = _dot(xb, w_ref[...])
    for c in range(4):
        zc_ref[c] = y[:, c * LANES:c * LANES + HEAD_DIM]
    o = NSA_CMP_COLS
    ks_ref[...] = y[:, o:o + LANES].astype(BF16)
    kw_ref[...] = y[:, o + LANES:o + 2 * LANES].astype(BF16)
    vt = _dot_nt(wvt_ref[...], xb)
    vst_ref[0] = vt[:LANES].astype(BF16)
    vwt_ref[0] = vt[LANES:].astype(BF16)


def _nsa_kv(x, w, wvt, seq):
    n = x.shape[0]
    per_b = seq // KV_TM
    tile = lambda: pl.BlockSpec((KV_TM, LANES), lambda i: (i, 0))
    tile_t = lambda: pl.BlockSpec((1, LANES, KV_TM), lambda i: (i // per_b, 0, i % per_b))
    return pl.pallas_call(
        _nsa_kv_kernel,
        grid=(n // KV_TM,),
        in_specs=[
            pl.BlockSpec((KV_TM, D_MODEL), lambda i: (i, 0)),
            pl.BlockSpec((D_MODEL, NSA_KV_COLS), lambda i: (0, 0)),
            pl.BlockSpec((2 * LANES, D_MODEL), lambda i: (0, 0)),
        ],
        out_specs=[pl.BlockSpec((4, KV_TM, HEAD_DIM), lambda i: (0, i, 0)), tile(), tile(), tile_t(), tile_t()],
        out_shape=[jax.ShapeDtypeStruct((4, n, HEAD_DIM), F32)]
        + [jax.ShapeDtypeStruct((n, LANES), BF16)] * 2
        + [jax.ShapeDtypeStruct((n // seq, LANES, seq), BF16)] * 2,
        compiler_params=_cparams(("parallel",)),
        name="nsa_kv",
    )(x, w, wvt)


CMP_HALF = CMP_STRIDE * HEAD_DIM


def _compress_kernel(z_ref, pos_ref, w1_ref, b1_ref, w2_ref, o_ref):
    r = z_ref[0]
    rows = r.shape[0]
    lo = _dot((r + pos_ref[0, 0:1, :]).astype(BF16), w1_ref[0, :CMP_HALF, :])
    hi = _dot((r + pos_ref[0, 1:2, :]).astype(BF16), w1_ref[0, CMP_HALF:, :])
    pre = lo + pltpu.roll(hi, rows - 1, 0) + b1_ref[0]
    o_ref[0] = _dot(jax.nn.gelu(pre).astype(BF16), w2_ref[0]).astype(BF16)


def _compress(z, pos, w1, b1, w2, bsz):
    nb, rows, _ = z.shape
    per_kv = NSA_GROUPS * bsz
    return pl.pallas_call(
        _compress_kernel,
        grid=(nb,),
        in_specs=[
            pl.BlockSpec((1, rows, CMP_HALF), lambda i: (i, 0, 0)),
            pl.BlockSpec((1, 2, CMP_HALF), lambda i: (i // per_kv, 0, 0)),
            pl.BlockSpec((1, 2 * CMP_HALF, CMP_HIDDEN), lambda i: (i // per_kv, 0, 0)),
            pl.BlockSpec((1, 1, CMP_HIDDEN), lambda i: (i // per_kv, 0, 0)),
            pl.BlockSpec((1, CMP_HIDDEN, HEAD_DIM), lambda i: (i // per_kv, 0, 0)),
        ],
        out_specs=pl.BlockSpec((1, rows, HEAD_DIM), lambda i: (i, 0, 0)),
        out_shape=jax.ShapeDtypeStruct((nb, rows, HEAD_DIM), BF16),
        compiler_params=_cparams(("parallel",)),
        name="nsa_compress",
    )(z, pos, w1, b1, w2)


NSA_TM = 512
NSA_GATE_COLS = NSA_GROUPS * LANES
NSA_IN_COLS = HQ + NSA_GATE_COLS + MEM_WIDTH


def _nsa_proj_kernel(x_ref, w_ref, q_ref, gt_ref, qm_ref):
    y = _dot(x_ref[...].astype(BF16), w_ref[...])
    q_ref[...] = y[:, :HQ].astype(BF16)
    gt_ref[...] = jax.nn.sigmoid(y[:, HQ:HQ + NSA_GATE_COLS])
    qm_ref[...] = y[:, HQ + NSA_GATE_COLS:].astype(BF16)


def _nsa_proj(x, w):
    n = x.shape[0]
    return pl.pallas_call(
        _nsa_proj_kernel,
        grid=(n // NSA_TM,),
        in_specs=[
            pl.BlockSpec((NSA_TM, D_MODEL), lambda i: (i, 0)),
            pl.BlockSpec((D_MODEL, NSA_IN_COLS), lambda i: (0, 0)),
        ],
        out_specs=[
            pl.BlockSpec((NSA_TM, HQ), lambda i: (i, 0)),
            pl.BlockSpec((NSA_TM, NSA_GATE_COLS), lambda i: (i, 0)),
            pl.BlockSpec((NSA_TM, MEM_WIDTH), lambda i: (i, 0)),
        ],
        out_shape=[
            jax.ShapeDtypeStruct((n, HQ), BF16),
            jax.ShapeDtypeStruct((n, NSA_GATE_COLS), F32),
            jax.ShapeDtypeStruct((n, MEM_WIDTH), BF16),
        ],
        compiler_params=_cparams(("parallel",)),
        name="nsa_proj",
    )(x, w)


NSA_TQ = 256
NSA_TK = 512
NSA_COLS = NSA_HPG * NSA_TQ
SEL_PER_TILE = NSA_TK // SEL_LEN


def _nsa_attn_kernel(slopes_ref, q_ref, gt_ref, kc_ref, vct_ref, ks_ref, vst_ref,
                     kwp_ref, kwc_ref, vwtp_ref, vwtc_ref, aggt_ref, o_ref,
                     qs_ref, selb_ref, m_ref, l_ref, acc_ref, oc_ref, ow_ref, *, n_cmp):
    g = pl.program_id(1)
    i = pl.program_id(2)
    t0 = i * NSA_TQ
    jmax = (t0 + NSA_TQ - 1) // NSA_TK
    hcols = lambda hh: slice(hh * NSA_TQ, (hh + 1) * NSA_TQ)
    slope = [slopes_ref[g * NSA_HPG + hh] for hh in range(NSA_HPG)]

    for hh in range(NSA_HPG):
        qs_ref[hcols(hh), :] = q_ref[:, hh * LANES:(hh + 1) * LANES]
    qs = qs_ref[...]

    def softmax_keys(s, keep):
        s = jnp.where(keep, s, NEG)
        mx = jnp.max(s, 0, keepdims=True)
        e = jnp.where(keep, jnp.exp(s - mx), 0.0)
        return (e / jnp.maximum(jnp.sum(e, 0, keepdims=True), 1e-30)).astype(BF16)

    nc = kc_ref.shape[1]
    blk_c = lax.broadcasted_iota(jnp.int32, (nc, NSA_TQ), 0)
    tq_c = t0 + lax.broadcasted_iota(jnp.int32, (nc, NSA_TQ), 1)
    keep_c = (blk_c * CMP_STRIDE + (CMP_LEN - 1) <= tq_c) & (blk_c < n_cmp)
    dist_c = tq_c.astype(F32) - (blk_c.astype(F32) * CMP_STRIDE + (CMP_LEN - 1) * 0.5)
    s_all = _dot_nt(kc_ref[0], qs)
    pt = jnp.concatenate(
        [softmax_keys(s_all[:, hcols(hh)] - slope[hh] * dist_c, keep_c) for hh in range(NSA_HPG)], 1)
    oc_ref[...] = _dot(vct_ref[0], pt)
    imp_h = _dot(aggt_ref[...], pt)
    imp = imp_h[:, hcols(0)]
    for hh in range(1, NSA_HPG):
        imp = imp + imp_h[:, hcols(hh)]

    blk = lax.broadcasted_iota(jnp.int32, (LANES, NSA_TQ), 0)
    tq_s = t0 + lax.broadcasted_iota(jnp.int32, (LANES, NSA_TQ), 1)
    cur = tq_s >> SEL_SHIFT
    forced = (blk == 0) | (blk == cur) | (blk == cur - 1)
    imp = jnp.where(forced, imp + FORCE_BONUS, imp)
    imp = jnp.where(blk * SEL_LEN <= tq_s, imp, NEG)
    blk_f = blk.astype(F32)
    selb = jnp.full((LANES, NSA_TQ), NEG, F32)
    for _ in range(SEL_TOPK):
        mx = jnp.max(imp, 0, keepdims=True)
        first = jnp.min(jnp.where(imp == mx, blk_f, float(LANES)), 0, keepdims=True)
        hit = blk_f == first
        selb = jnp.where(hit, 0.0, selb)
        imp = jnp.where(hit, -jnp.inf, imp)
    selb_ref[...] = selb

    wt = t0 // NSA_TK
    kw = jnp.concatenate([kwp_ref[...], kwc_ref[...]], 0)
    vwt = jnp.concatenate([vwtp_ref[0], vwtc_ref[0]], 1)
    pos_w = (wt - 1) * NSA_TK + lax.broadcasted_iota(jnp.int32, (2 * NSA_TK, NSA_TQ), 0)
    tq_w = t0 + lax.broadcasted_iota(jnp.int32, (2 * NSA_TK, NSA_TQ), 1)
    dist_w = tq_w - pos_w
    keep_w = (dist_w >= 0) & (dist_w < WINDOW) & (pos_w >= 0)
    dist_wf = dist_w.astype(F32)
    s_all = _dot_nt(kw, qs)
    pt = jnp.concatenate(
        [softmax_keys(s_all[:, hcols(hh)] - slope[hh] * dist_wf, keep_w) for hh in range(NSA_HPG)], 1)
    ow_ref[...] = _dot(vwt, pt)

    m_ref[...] = jnp.full(m_ref.shape, NEG, F32)
    l_ref[...] = jnp.zeros(l_ref.shape, F32)
    acc_ref[...] = jnp.zeros(acc_ref.shape, F32)

    def sel_tile(j, diag):
        start = pl.multiple_of(j * NSA_TK, NSA_TK)
        s_all = _dot_nt(ks_ref[pl.ds(start, NSA_TK), :], qs_ref[...])
        vt = vst_ref[0, :, pl.ds(start, NSA_TK)]
        rows = selb_ref[pl.ds(pl.multiple_of(j * SEL_PER_TILE, SEL_PER_TILE), SEL_PER_TILE), :]
        bias = jnp.concatenate(
            [jnp.broadcast_to(rows[b:b + 1], (SEL_LEN, NSA_TQ)) for b in range(SEL_PER_TILE)], 0)
        pos_k = j * NSA_TK + lax.broadcasted_iota(jnp.int32, (NSA_TK, NSA_TQ), 0)
        tq_k = t0 + lax.broadcasted_iota(jnp.int32, (NSA_TK, NSA_TQ), 1)
        dist = tq_k - pos_k
        if diag:
            bias = jnp.where(dist >= 0, bias, NEG)
        dist_f = dist.astype(F32)
        p_list = []
        for hh in range(NSA_HPG):
            c = hcols(hh)
            s = s_all[:, c] - slope[hh] * dist_f + bias
            m_old = m_ref[:, c]
            m_new = jnp.maximum(m_old, jnp.max(s, 0, keepdims=True))
            alpha = jnp.exp(m_old - m_new)
            p = jnp.exp(s - m_new)
            l_ref[:, c] = alpha * l_ref[:, c] + jnp.sum(p, 0, keepdims=True)
            acc_ref[:, c] = alpha * acc_ref[:, c]
            m_ref[:, c] = m_new
            p_list.append(p.astype(BF16))
        acc_ref[...] += _dot(vt, jnp.concatenate(p_list, 1))

    def body(j, carry):
        sel_tile(j, False)
        return carry

    lax.fori_loop(0, jmax, body, 0)
    sel_tile(jmax, True)

    gtt = gt_ref[...].T
    for hh in range(NSA_HPG):
        c = hcols(hh)
        r = hh * NSA_BRANCHES
        o_s = acc_ref[:, c] / jnp.maximum(l_ref[:, c], 1e-30)
        o = gtt[r:r + 1] * oc_ref[:, c] + gtt[r + 1:r + 2] * o_s + gtt[r + 2:r + 3] * ow_ref[:, c]
        o_ref[:, hh * LANES:(hh + 1) * LANES] = o.T.astype(BF16)


def _nsa_attn(slopes, q, gates, kc, vct, ks, vst, kw, vwt, agg_t, bsz, seq):
    nq = seq // NSA_TQ
    nk = seq // NSA_TK
    nc = kc.shape[1]
    n_cmp = (seq - CMP_LEN) // CMP_STRIDE + 1
    gw = NSA_HPG * LANES

    def wcur(i):
        return (i * NSA_TQ) // NSA_TK

    def wprev(i):
        return jnp.maximum(wcur(i) - 1, 0)

    return pl.pallas_call(
        functools.partial(_nsa_attn_kernel, n_cmp=n_cmp),
        grid=(bsz, NSA_GROUPS, nq),
        in_specs=[
            pl.BlockSpec(memory_space=pltpu.SMEM),
            pl.BlockSpec((NSA_TQ, gw), lambda b, g, i: (b * nq + i, g)),
            pl.BlockSpec((NSA_TQ, LANES), lambda b, g, i: (b * nq + i, g)),
            pl.BlockSpec((1, nc, LANES), lambda b, g, i: (b, 0, 0)),
            pl.BlockSpec((1, LANES, nc), lambda b, g, i: (b, 0, 0)),
            pl.BlockSpec((seq, LANES), lambda b, g, i: (b, 0)),
            pl.BlockSpec((1, LANES, seq), lambda b, g, i: (b, 0, 0)),
            pl.BlockSpec((NSA_TK, LANES), lambda b, g, i: (b * nk + wprev(i), 0)),
            pl.BlockSpec((NSA_TK, LANES), lambda b, g, i: (b * nk + wcur(i), 0)),
            pl.BlockSpec((1, LANES, NSA_TK), lambda b, g, i: (b, 0, wprev(i))),
            pl.BlockSpec((1, LANES, NSA_TK), lambda b, g, i: (b, 0, wcur(i))),
            pl.BlockSpec((LANES, nc), lambda b, g, i: (0, 0)),
        ],
        out_specs=pl.BlockSpec((NSA_TQ, gw), lambda b, g, i: (b * nq + i, g)),
        out_shape=jax.ShapeDtypeStruct((bsz * seq, HQ), BF16),
        scratch_shapes=[
            pltpu.VMEM((NSA_COLS, LANES), BF16),
            pltpu.VMEM((LANES, NSA_TQ), F32),
            pltpu.VMEM((1, NSA_COLS), F32),
            pltpu.VMEM((1, NSA_COLS), F32),
            pltpu.VMEM((LANES, NSA_COLS), F32),
            pltpu.VMEM((LANES, NSA_COLS), F32),
            pltpu.VMEM((LANES, NSA_COLS), F32),
        ],
        compiler_params=_cparams(("parallel", "parallel", "arbitrary")),
        name="nsa_attn",
    )(slopes, q, gates, kc, vct, ks, vst, kw, kw, vwt, vwt, agg_t)


def _alibi_slopes(n):
    def pow2(k):
        start = 2.0 ** (-8.0 / k)
        return [start ** (i + 1) for i in range(k)]
    if math.log2(n).is_integer():
        s = pow2(n)
    else:
        c = 2 ** math.floor(math.log2(n))
        s = pow2(c) + pow2(2 * c)[0::2][: n - c]
    return np.asarray(s, np.float32)


def _head_slots(w, width):
    r = w.shape[0]
    w3 = w.reshape(r, TOK_HEADS, width)
    return jnp.pad(w3, ((0, 0), (0, 0), (0, LANES - width))).reshape(r, HQ)


def _rope_pair(w_x1, w_x2):
    half = MLA_ROPE // 2
    r = w_x1.shape[0]
    z_lo = jnp.zeros((r, MLA_NOPE), w_x1.dtype)
    z_hi = jnp.zeros((r, LANES - MLA_NOPE - MLA_ROPE), w_x1.dtype)
    assert w_x1.shape[1] == half
    return (jnp.concatenate([z_lo, w_x1, w_x2, z_hi], 1),
            jnp.concatenate([z_lo, -w_x2, w_x1, z_hi], 1))


def _mla_weights(w_in, w_uq, w_ukv):
    half = MLA_ROPE // 2
    o = MLA_Q_RANK + MLA_KV_RANK
    kr, kr_sw = _rope_pair(w_in[:, o:o + half], w_in[:, o + half:o + MLA_ROPE])
    w_in_all = jnp.concatenate([w_in[:, :o], kr, kr_sw, w_in[:, o + MLA_ROPE:]], 1)
    wq3 = w_uq.reshape(MLA_Q_RANK, TOK_HEADS, MLA_NOPE + MLA_ROPE)
    nope, x1, x2 = wq3[..., :MLA_NOPE], wq3[..., MLA_NOPE:MLA_NOPE + half], wq3[..., MLA_NOPE + half:]
    z = jnp.zeros((MLA_Q_RANK, TOK_HEADS, LANES - MLA_NOPE - MLA_ROPE), w_uq.dtype)
    wq = jnp.concatenate([nope, x1, x2, z], -1).reshape(MLA_Q_RANK, HQ)
    wq_sw = jnp.concatenate([jnp.zeros_like(nope), -x2, x1, z], -1).reshape(MLA_Q_RANK, HQ)
    wkv3 = w_ukv.reshape(MLA_KV_RANK, TOK_HEADS, MLA_NOPE + HEAD_DIM)
    wk = _head_slots(wkv3[..., :MLA_NOPE].reshape(MLA_KV_RANK, -1), MLA_NOPE)
    wvt = wkv3[..., MLA_NOPE:].reshape(MLA_KV_RANK, TOK_WIDTH).T
    return (w_in_all.astype(BF16), jnp.concatenate([wq, wq_sw], 1).astype(BF16),
            wk.astype(BF16), wvt.astype(BF16))


def _rope_tables(seq):
    half = MLA_ROPE // 2
    freq = ROPE_THETA ** (-jnp.arange(half, dtype=F32) / half)
    ang = jnp.arange(seq).astype(F32)[:, None] * freq[None, :]
    cos, sin = jnp.cos(ang), jnp.sin(ang)
    ones = jnp.ones((seq, MLA_NOPE), F32)
    z_hi = jnp.zeros((seq, LANES - MLA_NOPE - MLA_ROPE), F32)
    cos_t = jnp.concatenate([ones, cos, cos, z_hi], 1)
    sin_t = jnp.concatenate([jnp.zeros_like(ones), sin, sin, z_hi], 1)
    return cos_t, sin_t


def _group_slots(w):
    r = w.shape[0]
    w4 = w.reshape(r, NSA_GROUPS, NSA_HPG, HEAD_DIM)
    tiles = []
    for g in range(NSA_GROUPS):
        pad = ((0, 0), (0, 0), (g * HEAD_DIM, LANES - (g + 1) * HEAD_DIM))
        tiles.append(jnp.pad(w4[:, g], pad))
    return jnp.stack(tiles, 1).reshape(r, HQ)


def _nsa_in_weights(w_in):
    scale = HEAD_DIM ** -0.5
    wq = _group_slots(w_in[:, :TOK_WIDTH] * scale)
    ng = NSA_HPG * NSA_BRANCHES
    wg = w_in[:, TOK_WIDTH:TOK_WIDTH + TOK_HEADS * NSA_BRANCHES].reshape(-1, NSA_GROUPS, ng)
    wg = jnp.pad(wg, ((0, 0), (0, 0), (0, LANES - ng))).reshape(-1, NSA_GATE_COLS)
    return jnp.concatenate([wq, wg, w_in[:, TOK_WIDTH + TOK_HEADS * NSA_BRANCHES:]], 1).astype(BF16)


def _nsa_kv_weights(w_kv):
    w5 = w_kv.reshape(D_MODEL, NSA_BRANCHES, 2, NSA_GROUPS * HEAD_DIM)
    cmp_cols = []
    for kv in range(2):
        for g in range(NSA_GROUPS):
            c = w5[:, 0, kv, g * HEAD_DIM:(g + 1) * HEAD_DIM]
            cmp_cols.append(jnp.pad(c, ((0, 0), (0, LANES - HEAD_DIM))))
    w = jnp.concatenate(cmp_cols + [w5[:, 1, 0], w5[:, 2, 0]], 1)
    wvt = jnp.concatenate([w5[:, 1, 1], w5[:, 2, 1]], 1).T
    return w.astype(BF16), wvt.astype(BF16)


def _out_weights_nsa(w_tok):
    return _group_slots(w_tok.T).T


def _agg_matrix(seq, rows):
    n_cmp = (seq - CMP_LEN) // CMP_STRIDE + 1
    n_sel = seq // SEL_LEN
    cmp_start = np.arange(n_cmp) * CMP_STRIDE
    sel_start = np.arange(n_sel) * SEL_LEN
    overlap = np.clip(np.minimum(cmp_start[:, None] + CMP_LEN, sel_start[None, :] + SEL_LEN)
                      - np.maximum(cmp_start[:, None], sel_start[None, :]), 0, None)
    agg_t = np.zeros((LANES, rows), np.float32)
    agg_t[:n_sel, :n_cmp] = (overlap / CMP_LEN).T
    return jnp.asarray(agg_t, BF16)


def kernel(x, mem, ln_g, ln_b, ffn_w_gu, ffn_w_down, w_mem_kv, w_out, mla_w_in, mla_q_norm_g, mla_kv_norm_g,
           mla_w_uq, mla_w_ukv, nsa_w_in, nsa_w_kv, cmp_pos, cmp_w1, cmp_b1, cmp_w2):
    bsz, seq, _ = x.shape
    n = bsz * seq
    assert seq % NSA_TK == 0 and seq % MLA_TQ == 0 and seq // SEL_LEN <= LANES
    h = x.reshape(n, D_MODEL)
    cos_t, sin_t = _rope_tables(seq)
    slopes = jnp.asarray(_alibi_slopes(TOK_HEADS))
    rows_c = seq // CMP_STRIDE
    agg_t = _agg_matrix(seq, rows_c)
    ln = lambda layer, k: (ln_g[layer, k][None, :], ln_b[layer, k][None, :])
    shared = None

    for layer in range(DEPTH):
        h = _ffn_ln(h, ffn_w_gu[layer, 0].astype(BF16), ffn_w_down[layer, 0].astype(BF16), *ln(layer, 0))
        km, vm = _mem_kv(mem, w_mem_kv[layer].astype(BF16))
        w_tok = w_out[layer, :TOK_WIDTH]
        w_memo = w_out[layer, TOK_WIDTH:].astype(BF16)
        if layer < N_A_LAYERS:
            w_in_all, wq, wk, wvt = _mla_weights(mla_w_in[layer], mla_w_uq[layer], mla_w_ukv[layer])
            q, k, vt, q_mem = _mla_proj(h, w_in_all, mla_q_norm_g[layer][None, :], mla_kv_norm_g[layer][None, :],
                                        wq, wk, wvt, cos_t, sin_t, seq)
            o_tok = _mla_attn(q, k, vt, bsz, seq)
            w_tok = w_tok.astype(BF16)
        else:
            q, gates, q_mem = _nsa_proj(h, _nsa_in_weights(nsa_w_in[layer - N_A_LAYERS]))
            o_tok = _nsa_attn(slopes, q, gates, *shared, agg_t, bsz, seq)
            w_tok = _out_weights_nsa(w_tok).astype(BF16)
        h = _mix_out(h, o_tok, q_mem, km, vm, w_tok, w_memo, *ln(layer, 1), seq)
        h = _ffn_ln(h, ffn_w_gu[layer, 1].astype(BF16), ffn_w_down[layer, 1].astype(BF16), *ln(layer, 2))
        if layer == N_A_LAYERS - 1:
            zc, ks, kw, vst, vwt = _nsa_kv(h, *_nsa_kv_weights(nsa_w_kv), seq)
            z = zc.reshape(4 * bsz, rows_c, CMP_HALF)
            c = _compress(z, cmp_pos.reshape(2, 2, CMP_HALF), cmp_w1.astype(BF16), cmp_b1[:, None, :],
                          cmp_w2.astype(BF16), bsz)
            c = c.reshape(2, NSA_GROUPS, bsz, rows_c, HEAD_DIM)
            kc = jnp.concatenate([c[0, g] for g in range(NSA_GROUPS)], -1)
            vct = jnp.swapaxes(jnp.concatenate([c[1, g] for g in range(NSA_GROUPS)], -1), 1, 2)
            shared = (kc, vct, ks, vst, kw, vwt)
    return h.reshape(bsz, seq, D_MODEL)
```

```python
import functools
import math

import numpy as np
import jax
import jax.numpy as jnp
from jax import lax
from jax.experimental import pallas as pl
from jax.experimental.pallas import tpu as pltpu

F32 = jnp.float32
BF16 = jnp.bfloat16

D_MODEL = 1024
DEPTH = 4
N_A_LAYERS = DEPTH // 2

TOK_HEADS = 12
HEAD_DIM = 64
MEM_HEADS = 4
MEM_HEAD_DIM = 64
TOK_WIDTH = TOK_HEADS * HEAD_DIM
MEM_WIDTH = MEM_HEADS * MEM_HEAD_DIM

MLA_Q_RANK = 256
MLA_KV_RANK = 128
MLA_NOPE = 64
MLA_ROPE = 32
ROPE_THETA = 10000.0

NSA_GROUPS = 2
NSA_HPG = TOK_HEADS // NSA_GROUPS
NSA_BRANCHES = 3
CMP_LEN = 32
CMP_STRIDE = 16
CMP_HIDDEN = 256
SEL_LEN = 64
SEL_SHIFT = 6
SEL_TOPK = 16
WINDOW = 512

D_FF = 2816
DN_ALPHA = (2 * DEPTH) ** 0.25
LN_EPS = 1e-5
RMS_EPS = 1e-6
NEG = -1e30
FORCE_BONUS = 1e4

LANES = 128
VMEM_LIMIT = 56 * 1024 * 1024


def _cparams(sem):
    return pltpu.CompilerParams(dimension_semantics=sem, vmem_limit_bytes=VMEM_LIMIT)


def _layer_norm(y, g, b):
    mu = jnp.mean(y, -1, keepdims=True)
    yc = y - mu
    var = jnp.mean(yc * yc, -1, keepdims=True)
    return yc * lax.rsqrt(var + LN_EPS) * g + b


def _rms_norm(x, g):
    return x * lax.rsqrt(jnp.mean(x * x, -1, keepdims=True) + RMS_EPS) * g


def _dot(a, b):
    return jnp.dot(a, b, preferred_element_type=F32)


def _dot_nt(a, b):
    return lax.dot_general(a, b, (((1,), (1,)), ((), ())), preferred_element_type=F32)


FFN_TM = 1024
FFN_TF = 1408


def _ffn_kernel(x_ref, wgu_ref, wd_ref, g_ref, b_ref, o_ref, *, nj):
    j = pl.program_id(1)
    gu = _dot(x_ref[...].astype(BF16), wgu_ref[...])
    gate = gu[:, :FFN_TF]
    up = gu[:, FFN_TF:]
    h = (gate * jax.nn.sigmoid(gate) * up).astype(BF16)
    part = _dot(h, wd_ref[...])

    @pl.when(j == 0)
    def _():
        o_ref[...] = part

    @pl.when(j > 0)
    def _():
        o_ref[...] += part

    @pl.when(j == nj - 1)
    def _():
        y = DN_ALPHA * x_ref[...] + 0.5 * o_ref[...]
        o_ref[...] = _layer_norm(y, g_ref[...], b_ref[...])


def _ffn_weights(w_gu, w_down):
    nj = D_FF // FFN_TF
    w = w_gu.astype(BF16).reshape(D_MODEL, 2, nj, FFN_TF)
    return jnp.swapaxes(w, 1, 2).reshape(D_MODEL, 2 * D_FF), w_down.astype(BF16)


def _ffn_ln(x, w_gu, w_down, g, b):
    n = x.shape[0]
    nj = D_FF // FFN_TF
    return pl.pallas_call(
        functools.partial(_ffn_kernel, nj=nj),
        grid=(n // FFN_TM, nj),
        in_specs=[
            pl.BlockSpec((FFN_TM, D_MODEL), lambda i, j: (i, 0)),
            pl.BlockSpec((D_MODEL, 2 * FFN_TF), lambda i, j: (0, j)),
            pl.BlockSpec((FFN_TF, D_MODEL), lambda i, j: (j, 0)),
            pl.BlockSpec((1, D_MODEL), lambda i, j: (0, 0)),
            pl.BlockSpec((1, D_MODEL), lambda i, j: (0, 0)),
        ],
        out_specs=pl.BlockSpec((FFN_TM, D_MODEL), lambda i, j: (i, 0)),
        out_shape=jax.ShapeDtypeStruct((n, D_MODEL), F32),
        compiler_params=_cparams(("parallel", "arbitrary")),
        name="ffn_ln",
    )(x, w_gu, w_down, g, b)


def _memkv_kernel(mem_ref, w_ref, km_ref, v_ref):
    kv = _dot(mem_ref[0].astype(BF16), w_ref[...])
    k = kv[:, :MEM_WIDTH]
    v_ref[0] = kv[:, MEM_WIDTH:].astype(BF16)
    lane = lax.broadcasted_iota(jnp.int32, k.shape, 1)
    for h in range(MEM_HEADS):
        in_head = (lane >= h * MEM_HEAD_DIM) & (lane < (h + 1) * MEM_HEAD_DIM)
        km_ref[0, h] = jnp.where(in_head, k, 0.0).astype(BF16)


def _mem_kv(mem, w_mem_kv):
    bsz, m, _ = mem.shape
    return pl.pallas_call(
        _memkv_kernel,
        grid=(bsz,),
        in_specs=[
            pl.BlockSpec((1, m, D_MODEL), lambda b: (b, 0, 0)),
            pl.BlockSpec((D_MODEL, 2 * MEM_WIDTH), lambda b: (0, 0)),
        ],
        out_specs=[
            pl.BlockSpec((1, MEM_HEADS, m, MEM_WIDTH), lambda b: (b, 0, 0, 0)),
            pl.BlockSpec((1, m, MEM_WIDTH), lambda b: (b, 0, 0)),
        ],
        out_shape=[
            jax.ShapeDtypeStruct((bsz, MEM_HEADS, m, MEM_WIDTH), BF16),
            jax.ShapeDtypeStruct((bsz, m, MEM_WIDTH), BF16),
        ],
        compiler_params=_cparams(("parallel",)),
        name="mem_kv",
    )(mem, w_mem_kv)


OUT_TM = 512


def _mix_out_kernel(x_ref, ot_ref, qm_ref, km_ref, vm_ref, wt_ref, wm_ref, g_ref, b_ref, o_ref):
    qm = qm_ref[...]
    vm = vm_ref[0]
    lane = lax.broadcasted_iota(jnp.int32, (qm.shape[0], MEM_WIDTH), 1)
    o_mem = jnp.zeros((qm.shape[0], MEM_WIDTH), F32)
    for h in range(MEM_HEADS):
        s = _dot_nt(qm, km_ref[0, h]) * (MEM_HEAD_DIM ** -0.5)
        m = jnp.max(s, -1, keepdims=True)
        e = jnp.exp(s - m)
        p = (e / jnp.sum(e, -1, keepdims=True)).astype(BF16)
        pv = _dot(p, vm)
        in_head = (lane >= h * MEM_HEAD_DIM) & (lane < (h + 1) * MEM_HEAD_DIM)
        o_mem = jnp.where(in_head, pv, o_mem)
    mix = _dot(ot_ref[...], wt_ref[...]) + _dot(o_mem.astype(BF16), wm_ref[...])
    y = DN_ALPHA * x_ref[...] + mix
    o_ref[...] = _layer_norm(y, g_ref[...], b_ref[...])


def _mix_out(x, o_tok, q_mem, km, vm, w_tok, w_mem, g, b, seq):
    n = x.shape[0]
    kt = o_tok.shape[1]
    m = vm.shape[1]
    per_b = seq // OUT_TM
    return pl.pallas_call(
        _mix_out_kernel,
        grid=(n // OUT_TM,),
        in_specs=[
            pl.BlockSpec((OUT_TM, D_MODEL), lambda i: (i, 0)),
            pl.BlockSpec((OUT_TM, kt), lambda i: (i, 0)),
            pl.BlockSpec((OUT_TM, MEM_WIDTH), lambda i: (i, 0)),
            pl.BlockSpec((1, MEM_HEADS, m, MEM_WIDTH), lambda i: (i // per_b, 0, 0, 0)),
            pl.BlockSpec((1, m, MEM_WIDTH), lambda i: (i // per_b, 0, 0)),
            pl.BlockSpec((kt, D_MODEL), lambda i: (0, 0)),
            pl.BlockSpec((MEM_WIDTH, D_MODEL), lambda i: (0, 0)),
            pl.BlockSpec((1, D_MODEL), lambda i: (0, 0)),
            pl.BlockSpec((1, D_MODEL), lambda i: (0, 0)),
        ],
        out_specs=pl.BlockSpec((OUT_TM, D_MODEL), lambda i: (i, 0)),
        out_shape=jax.ShapeDtypeStruct((n, D_MODEL), F32),
        compiler_params=_cparams(("parallel",)),
        name="mix_out",
    )(x, o_tok, q_mem, km, vm, w_tok, w_mem, g, b)


MLA_TM = 512
MLA_IN_COLS = MLA_Q_RANK + MLA_KV_RANK + 2 * LANES + MEM_WIDTH
HQ = TOK_HEADS * LANES


def _mla_proj_kernel(x_ref, win_ref, qg_ref, kvg_ref, wq_ref, wk_ref, wvt_ref, cos_ref, sin_ref,
                     q_ref, k_ref, vt_ref, qm_ref):
    xb = x_ref[...].astype(BF16)
    hh = _dot(xb, win_ref[...])
    c_q = hh[:, :MLA_Q_RANK]
    c_kv = hh[:, MLA_Q_RANK:MLA_Q_RANK + MLA_KV_RANK]
    o = MLA_Q_RANK + MLA_KV_RANK
    kr = hh[:, o:o + LANES]
    kr_sw = hh[:, o + LANES:o + 2 * LANES]
    qm_ref[...] = hh[:, o + 2 * LANES:].astype(BF16)
    cos = cos_ref[...]
    sin = sin_ref[...]
    qq = _dot(_rms_norm(c_q, qg_ref[...]).astype(BF16), wq_ref[...])
    ckv = _rms_norm(c_kv, kvg_ref[...]).astype(BF16)
    kk = _dot(ckv, wk_ref[...])
    vt_ref[0] = _dot_nt(wvt_ref[...], ckv).astype(BF16)
    kr_rot = kr * cos + kr_sw * sin
    for h in range(TOK_HEADS):
        sl = slice(h * LANES, (h + 1) * LANES)
        sl2 = slice(HQ + h * LANES, HQ + (h + 1) * LANES)
        q_ref[:, sl] = (qq[:, sl] * cos + qq[:, sl2] * sin).astype(BF16)
        k_ref[:, sl] = (kk[:, sl] + kr_rot).astype(BF16)


def _mla_proj(x, w_in, qg, kvg, w_q, w_k, w_vt, cos_t, sin_t, seq):
    n = x.shape[0]
    per_b = seq // MLA_TM
    full = lambda shape: pl.BlockSpec(shape, lambda i: (0, 0))
    return pl.pallas_call(
        _mla_proj_kernel,
        grid=(n // MLA_TM,),
        in_specs=[
            pl.BlockSpec((MLA_TM, D_MODEL), lambda i: (i, 0)),
            full((D_MODEL, MLA_IN_COLS)),
            full((1, MLA_Q_RANK)),
            full((1, MLA_KV_RANK)),
            full((MLA_Q_RANK, 2 * HQ)),
            full((MLA_KV_RANK, HQ)),
            full((TOK_WIDTH, MLA_KV_RANK)),
            pl.BlockSpec((MLA_TM, LANES), lambda i: (i % per_b, 0)),
            pl.BlockSpec((MLA_TM, LANES), lambda i: (i % per_b, 0)),
        ],
        out_specs=[
            pl.BlockSpec((MLA_TM, HQ), lambda i: (i, 0)),
            pl.BlockSpec((MLA_TM, HQ), lambda i: (i, 0)),
            pl.BlockSpec((1, TOK_WIDTH, MLA_TM), lambda i: (i // per_b, 0, i % per_b)),
            pl.BlockSpec((MLA_TM, MEM_WIDTH), lambda i: (i, 0)),
        ],
        out_shape=[
            jax.ShapeDtypeStruct((n, HQ), BF16),
            jax.ShapeDtypeStruct((n, HQ), BF16),
            jax.ShapeDtypeStruct((n // seq, TOK_WIDTH, seq), BF16),
            jax.ShapeDtypeStruct((n, MEM_WIDTH), BF16),
        ],
        compiler_params=_cparams(("parallel",)),
        name="mla_proj",
    )(x, w_in, qg, kvg, w_q, w_k, w_vt, cos_t, sin_t)


MLA_TQ = 512
MLA_TK = 512
MLA_SCALE = (MLA_NOPE + MLA_ROPE) ** -0.5
MLA_EXP2_SCALE = MLA_SCALE * math.log2(math.e)


def _mla_attn_kernel(q_ref, k_ref, vt_ref, o_ref, m_ref, l_ref, acc_ref, s0_ref, s1_ref):
    i = pl.program_id(2)
    m_ref[...] = jnp.full(m_ref.shape, NEG, F32)
    l_ref[...] = jnp.zeros(l_ref.shape, F32)
    acc_ref[...] = jnp.zeros(acc_ref.shape, F32)
    s_refs = (s0_ref, s1_ref)

    def scores(j, hd):
        start = pl.multiple_of(j * MLA_TK, MLA_TK)
        sl = slice(hd * LANES, (hd + 1) * LANES)
        s_refs[hd][...] = _dot_nt(k_ref[pl.ds(start, MLA_TK), sl], q_ref[:, sl])

    def update(j, hd, masked):
        start = pl.multiple_of(j * MLA_TK, MLA_TK)
        rows = slice(hd * HEAD_DIM, (hd + 1) * HEAD_DIM)
        s = s_refs[hd][...]
        if masked:
            krow = lax.broadcasted_iota(jnp.int32, (MLA_TK, MLA_TQ), 0)
            qcol = lax.broadcasted_iota(jnp.int32, (MLA_TK, MLA_TQ), 1)
            keep = krow <= qcol
            s = jnp.where(keep, s, NEG)
        m_old = m_ref[hd]
        m_new = jnp.maximum(m_old, jnp.max(s, 0, keepdims=True))
        alpha = jnp.exp2((m_old - m_new) * MLA_EXP2_SCALE)
        p = jnp.exp2((s - m_new) * MLA_EXP2_SCALE)
        if masked:
            p = jnp.where(keep, p, 0.0)
        l_ref[hd] = alpha * l_ref[hd] + jnp.sum(p, 0, keepdims=True)
        acc_ref[rows, :] = alpha * acc_ref[rows, :] + _dot(vt_ref[0, rows, pl.ds(start, MLA_TK)], p.astype(BF16))
        m_ref[hd] = m_new

    scores(0, 0)

    def body(j, carry):
        scores(j, 1)
        update(j, 0, False)
        scores(j + 1, 0)
        update(j, 1, False)
        return carry

    lax.fori_loop(0, i, body, 0)
    scores(i, 1)
    update(i, 0, True)
    update(i, 1, True)
    inv = jnp.concatenate(
        [jnp.broadcast_to(1.0 / jnp.maximum(l_ref[hd], 1e-30), (HEAD_DIM, MLA_TQ)) for hd in range(2)], 0)
    o_ref[...] = (acc_ref[...] * inv).T.astype(BF16)


def _mla_attn(q, k, vt, bsz, seq):
    assert MLA_TQ == MLA_TK
    nq = seq // MLA_TQ
    return pl.pallas_call(
        _mla_attn_kernel,
        grid=(bsz, TOK_HEADS // 2, nq),
        in_specs=[
            pl.BlockSpec((MLA_TQ, 2 * LANES), lambda b, p, i: (b * nq + i, p)),
            pl.BlockSpec((seq, 2 * LANES), lambda b, p, i: (b, p)),
            pl.BlockSpec((1, 2 * HEAD_DIM, seq), lambda b, p, i: (b, p, 0)),
        ],
        out_specs=pl.BlockSpec((MLA_TQ, LANES), lambda b, p, i: (b * nq + i, p)),
        out_shape=jax.ShapeDtypeStruct((bsz * seq, TOK_WIDTH), BF16),
        scratch_shapes=[
            pltpu.VMEM((2, 1, MLA_TQ), F32),
            pltpu.VMEM((2, 1, MLA_TQ), F32),
            pltpu.VMEM((2 * HEAD_DIM, MLA_TQ), F32),
            pltpu.VMEM((MLA_TK, MLA_TQ), F32),
            pltpu.VMEM((MLA_TK, MLA_TQ), F32),
        ],
        compiler_params=_cparams(("parallel", "parallel", "arbitrary")),
        name="mla_attn",
    )(q, k, vt)


KV_TM = 512
NSA_CMP_COLS = 4 * LANES
KSX = 2 * LANES
NSA_KS_COLS = NSA_GROUPS * KSX
NSA_KW_COLS = NSA_GROUPS * LANES
NSA_KV_COLS = NSA_CMP_COLS + NSA_KS_COLS + NSA_KW_COLS


def _nsa_kv_kernel(x_ref, w_ref, wvt_ref, kaux_ref, kwaux_ref, zc_ref, ks_ref, kw_ref, vst_ref, vwt_ref):
    xb = x_ref[...].astype(BF16)
    y = _dot(xb, w_ref[...])
    for c in range(4):
        zc_ref[c] = y[:, c * LANES:c * LANES + HEAD_DIM]
    o = NSA_CMP_COLS
    kaux = kaux_ref[...]
    kwaux = kwaux_ref[...]
    for g in range(NSA_GROUPS):
        ks_ref[:, g * KSX:(g + 1) * KSX] = (y[:, o + g * KSX:o + (g + 1) * KSX] + kaux).astype(BF16)
        ow = o + NSA_KS_COLS + g * LANES
        kw_ref[:, g * LANES:(g + 1) * LANES] = (y[:, ow:ow + LANES] + kwaux).astype(BF16)
    vt = _dot_nt(wvt_ref[...], xb)
    vst_ref[0] = vt[:LANES].astype(BF16)
    vwt_ref[0] = vt[LANES:].astype(BF16)


def _nsa_kv(x, w, wvt, kaux, kwaux, seq):
    n = x.shape[0]
    per_b = seq // KV_TM
    tile_t = lambda: pl.BlockSpec((1, LANES, KV_TM), lambda i: (i // per_b, 0, i % per_b))
    return pl.pallas_call(
        _nsa_kv_kernel,
        grid=(n // KV_TM,),
        in_specs=[
            pl.BlockSpec((KV_TM, D_MODEL), lambda i: (i, 0)),
            pl.BlockSpec((D_MODEL, NSA_KV_COLS), lambda i: (0, 0)),
            pl.BlockSpec((2 * LANES, D_MODEL), lambda i: (0, 0)),
            pl.BlockSpec((KV_TM, KSX), lambda i: (i % per_b, 0)),
            pl.BlockSpec((KV_TM, LANES), lambda i: (i % per_b, 0)),
        ],
        out_specs=[
            pl.BlockSpec((4, KV_TM, HEAD_DIM), lambda i: (0, i, 0)),
            pl.BlockSpec((KV_TM, NSA_KS_COLS), lambda i: (i, 0)),
            pl.BlockSpec((KV_TM, NSA_KW_COLS), lambda i: (i, 0)),
            tile_t(),
            tile_t(),
        ],
        out_shape=[
            jax.ShapeDtypeStruct((4, n, HEAD_DIM), F32),
            jax.ShapeDtypeStruct((n, NSA_KS_COLS), BF16),
            jax.ShapeDtypeStruct((n, NSA_KW_COLS), BF16),
            jax.ShapeDtypeStruct((n // seq, LANES, seq), BF16),
            jax.ShapeDtypeStruct((n // seq, LANES, seq), BF16),
        ],
        compiler_params=_cparams(("parallel",)),
        name="nsa_kv",
    )(x, w, wvt, kaux, kwaux)


CMP_HALF = CMP_STRIDE * HEAD_DIM


def _compress_kernel(z_ref, pos_ref, w1_ref, b1_ref, w2_ref, o_ref):
    r = z_ref[0]
    rows = r.shape[0]
    lo = _dot((r + pos_ref[0, 0:1, :]).astype(BF16), w1_ref[0, :CMP_HALF, :])
    hi = _dot((r + pos_ref[0, 1:2, :]).astype(BF16), w1_ref[0, CMP_HALF:, :])
    pre = lo + pltpu.roll(hi, rows - 1, 0) + b1_ref[0]
    o_ref[0] = _dot(jax.nn.gelu(pre).astype(BF16), w2_ref[0]).astype(BF16)


def _compress(z, pos, w1, b1, w2, bsz):
    nb, rows, _ = z.shape
    per_kv = NSA_GROUPS * bsz
    return pl.pallas_call(
        _compress_kernel,
        grid=(nb,),
        in_specs=[
            pl.BlockSpec((1, rows, CMP_HALF), lambda i: (i, 0, 0)),
            pl.BlockSpec((1, 2, CMP_HALF), lambda i: (i // per_kv, 0, 0)),
            pl.BlockSpec((1, 2 * CMP_HALF, CMP_HIDDEN), lambda i: (i // per_kv, 0, 0)),
            pl.BlockSpec((1, 1, CMP_HIDDEN), lambda i: (i // per_kv, 0, 0)),
            pl.BlockSpec((1, CMP_HIDDEN, HEAD_DIM), lambda i: (i // per_kv, 0, 0)),
        ],
        out_specs=pl.BlockSpec((1, rows, HEAD_DIM), lambda i: (i, 0, 0)),
        out_shape=jax.ShapeDtypeStruct((nb, rows, HEAD_DIM), BF16),
        compiler_params=_cparams(("parallel",)),
        name="nsa_compress",
    )(z, pos, w1, b1, w2)


NSA_TM = 512
NSA_GATE_COLS = NSA_GROUPS * LANES
NSA_IN_COLS = HQ + NSA_GATE_COLS + MEM_WIDTH


def _nsa_proj_kernel(x_ref, w_ref, qaux_ref, q_ref, gt_ref, qm_ref):
    y = _dot(x_ref[...].astype(BF16), w_ref[...])
    q_ref[...] = (y[:, :HQ] + qaux_ref[...]).astype(BF16)
    gt_ref[...] = jax.nn.sigmoid(y[:, HQ:HQ + NSA_GATE_COLS])
    qm_ref[...] = y[:, HQ + NSA_GATE_COLS:].astype(BF16)


def _nsa_proj(x, w, qaux):
    n = x.shape[0]
    return pl.pallas_call(
        _nsa_proj_kernel,
        grid=(n // NSA_TM,),
        in_specs=[
            pl.BlockSpec((NSA_TM, D_MODEL), lambda i: (i, 0)),
            pl.BlockSpec((D_MODEL, NSA_IN_COLS), lambda i: (0, 0)),
            pl.BlockSpec((1, HQ), lambda i: (0, 0)),
        ],
        out_specs=[
            pl.BlockSpec((NSA_TM, HQ), lambda i: (i, 0)),
            pl.BlockSpec((NSA_TM, NSA_GATE_COLS), lambda i: (i, 0)),
            pl.BlockSpec((NSA_TM, MEM_WIDTH), lambda i: (i, 0)),
        ],
        out_shape=[
            jax.ShapeDtypeStruct((n, HQ), BF16),
            jax.ShapeDtypeStruct((n, NSA_GATE_COLS), F32),
            jax.ShapeDtypeStruct((n, MEM_WIDTH), BF16),
        ],
        compiler_params=_cparams(("parallel",)),
        name="nsa_proj",
    )(x, w, qaux)


NSA_TQ = 256
NSA_TK = 512
NSA_COLS = NSA_HPG * NSA_TQ


def _nsa_attn_kernel(slopes_ref, q_ref, gt_ref, kc_ref, vct_ref, ks_ref, vst_ref,
                     kw0_ref, kw1_ref, kw2_ref, vwt0_ref, vwt1_ref, vwt2_ref, aggt_ref, whi_ref, o_ref,
                     qs_ref, m_ref, l_ref, acc_ref, oc_ref, ow_ref, sa_ref, sb_ref, *, n_cmp):
    g = pl.program_id(1)
    i = pl.program_id(2)
    t0 = i * NSA_TQ
    jmax = (t0 + NSA_TQ - 1) // NSA_TK
    hcols = lambda hh: slice(hh * NSA_TQ, (hh + 1) * NSA_TQ)
    slope = [slopes_ref[g * NSA_HPG + hh] for hh in range(NSA_HPG)]

    for hh in range(NSA_HPG):
        qs_ref[hcols(hh), :LANES] = q_ref[:, hh * LANES:(hh + 1) * LANES]
    qs = qs_ref[:, :LANES]

    def softmax_keys(s, bias, any_valid=None):
        s = s + bias
        e = jnp.exp(s - jnp.max(s, 0, keepdims=True))
        inv = 1.0 / jnp.maximum(jnp.sum(e, 0, keepdims=True), 1e-30)
        if any_valid is not None:
            inv = jnp.where(any_valid, inv, 0.0)
        return (e * inv).astype(BF16)

    nc = kc_ref.shape[2]
    blk_c = lax.broadcasted_iota(jnp.int32, (nc, NSA_TQ), 0)
    tq_c = t0 + lax.broadcasted_iota(jnp.int32, (nc, NSA_TQ), 1)
    keep_c = (blk_c * CMP_STRIDE + (CMP_LEN - 1) <= tq_c) & (blk_c < n_cmp)
    bias_c = jnp.where(keep_c, 0.0, NEG)
    valid_c = t0 + lax.broadcasted_iota(jnp.int32, (1, NSA_TQ), 1) >= CMP_LEN - 1
    s_all = _dot_nt(kc_ref[0, 0], qs)
    pt = jnp.concatenate([softmax_keys(s_all[:, hcols(hh)], bias_c, valid_c) for hh in range(NSA_HPG)], 1)
    oc_ref[...] = _dot(vct_ref[0], pt)
    imp_h = _dot(aggt_ref[...], pt)
    imp = imp_h[:, hcols(0)]
    for hh in range(1, NSA_HPG):
        imp = imp + imp_h[:, hcols(hh)]

    blk = lax.broadcasted_iota(jnp.int32, (LANES, NSA_TQ), 0)
    tq_s = t0 + lax.broadcasted_iota(jnp.int32, (LANES, NSA_TQ), 1)
    cur = tq_s >> SEL_SHIFT
    forced = (blk == 0) | (blk == cur) | (blk == cur - 1)
    imp = jnp.where(forced, imp + FORCE_BONUS, imp)
    imp = jnp.where(blk * SEL_LEN <= tq_s, imp, NEG)
    blk_f = blk.astype(F32)
    selb = jnp.full((LANES, NSA_TQ), NEG, F32)
    for _ in range(SEL_TOPK):
        mx = jnp.max(imp, 0, keepdims=True)
        first = jnp.min(jnp.where(imp == mx, blk_f, float(LANES)), 0, keepdims=True)
        hit = blk_f == first
        selb = jnp.where(hit, 0.0, selb)
        imp = jnp.where(hit, -jnp.inf, imp)
    selb_q = selb.T.astype(BF16)
    for hh in range(NSA_HPG):
        qs_ref[hcols(hh), LANES:] = selb_q

    kw = jnp.concatenate([kw0_ref[...], kw1_ref[...], kw2_ref[...]], 0) + whi_ref[...]
    vwt = jnp.concatenate([vwt0_ref[0], vwt1_ref[0], vwt2_ref[0]], 1)
    pos_w = t0 - 2 * NSA_TQ + lax.broadcasted_iota(jnp.int32, (3 * NSA_TQ, NSA_TQ), 0)
    tq_w = t0 + lax.broadcasted_iota(jnp.int32, (3 * NSA_TQ, NSA_TQ), 1)
    dist_w = tq_w - pos_w
    bias_w = jnp.where((dist_w >= 0) & (dist_w < WINDOW) & (pos_w >= 0), 0.0, NEG)
    s_all = _dot_nt(kw, qs)
    pt = jnp.concatenate([softmax_keys(s_all[:, hcols(hh)], bias_w) for hh in range(NSA_HPG)], 1)
    ow_ref[...] = _dot(vwt, pt)

    m_ref[...] = jnp.full(m_ref.shape, NEG, F32)
    l_ref[...] = jnp.zeros(l_ref.shape, F32)
    acc_ref[...] = jnp.zeros(acc_ref.shape, F32)
    s_refs = (sa_ref, sb_ref)
    half = NSA_HPG // 2
    tq_row = (t0 + lax.broadcasted_iota(jnp.int32, (1, NSA_TQ), 1)).astype(F32)

    def scores(j, st):
        start = pl.multiple_of(j * NSA_TK, NSA_TK)
        q_rows = slice(st * half * NSA_TQ, (st + 1) * half * NSA_TQ)
        s_refs[st][...] = _dot_nt(ks_ref[pl.ds(start, NSA_TK), :], qs_ref[q_rows, :])

    def update(j, st, diag):
        start = pl.multiple_of(j * NSA_TK, NSA_TK)
        if diag:
            pos_k = j * NSA_TK + lax.broadcasted_iota(jnp.int32, (NSA_TK, NSA_TQ), 0)
            tq_k = t0 + lax.broadcasted_iota(jnp.int32, (NSA_TK, NSA_TQ), 1)
            keep = pos_k <= tq_k
        rel = tq_row - (j * NSA_TK).astype(F32)
        p_list = []
        for hl in range(half):
            hh = st * half + hl
            c = hcols(hh)
            s = s_refs[st][:, hcols(hl)]
            if diag:
                s = jnp.where(keep, s, NEG)
            shift = slope[hh] * rel
            m_old = m_ref[:, c]
            m_new = jnp.maximum(m_old, jnp.max(s, 0, keepdims=True) - shift)
            alpha = jnp.exp(m_old - m_new)
            p = jnp.exp(s - (m_new + shift))
            l_ref[:, c] = alpha * l_ref[:, c] + jnp.sum(p, 0, keepdims=True)
            acc_ref[:, c] = alpha * acc_ref[:, c]
            m_ref[:, c] = m_new
            p_list.append(p.astype(BF16))
        cs = slice(st * half * NSA_TQ, (st + 1) * half * NSA_TQ)
        acc_ref[:, cs] += _dot(vst_ref[0, :, pl.ds(start, NSA_TK)], jnp.concatenate(p_list, 1))

    scores(0, 0)

    def body(j, carry):
        scores(j, 1)
        update(j, 0, False)
        scores(j + 1, 0)
        update(j, 1, False)
        return carry

    lax.fori_loop(0, jmax, body, 0)
    scores(jmax, 1)
    update(jmax, 0, True)
    update(jmax, 1, True)

    gtt = gt_ref[...].T
    for hh in range(NSA_HPG):
        c = hcols(hh)
        r = hh * NSA_BRANCHES
        o_s = acc_ref[:, c] / jnp.maximum(l_ref[:, c], 1e-30)
        o = gtt[r:r + 1] * oc_ref[:, c] + gtt[r + 1:r + 2] * o_s + gtt[r + 2:r + 3] * ow_ref[:, c]
        o_ref[:, hh * LANES:(hh + 1) * LANES] = o.T.astype(BF16)


def _nsa_attn(slopes, q, gates, kc, vct, ks, vst, kw, vwt, agg_t, bsz, seq):
    assert WINDOW <= 2 * NSA_TQ and NSA_HPG % 2 == 0
    nq = seq // NSA_TQ
    nc = kc.shape[2]
    n_cmp = (seq - CMP_LEN) // CMP_STRIDE + 1
    gw = NSA_HPG * LANES
    kw_spec = lambda d: pl.BlockSpec((NSA_TQ, LANES), lambda b, g, i: (b * nq + jnp.maximum(i - d, 0), g))
    vwt_spec = lambda d: pl.BlockSpec((1, LANES, NSA_TQ), lambda b, g, i: (b, 0, jnp.maximum(i - d, 0)))
    return pl.pallas_call(
        functools.partial(_nsa_attn_kernel, n_cmp=n_cmp),
        grid=(bsz, NSA_GROUPS, nq),
        in_specs=[
            pl.BlockSpec(memory_space=pltpu.SMEM),
            pl.BlockSpec((NSA_TQ, gw), lambda b, g, i: (b * nq + i, g)),
            pl.BlockSpec((NSA_TQ, LANES), lambda b, g, i: (b * nq + i, g)),
            pl.BlockSpec((1, 1, nc, LANES), lambda b, g, i: (b, g, 0, 0)),
            pl.BlockSpec((1, LANES, nc), lambda b, g, i: (b, 0, 0)),
            pl.BlockSpec((seq, KSX), lambda b, g, i: (b, g)),
            pl.BlockSpec((1, LANES, seq), lambda b, g, i: (b, 0, 0)),
            kw_spec(2), kw_spec(1), kw_spec(0),
            vwt_spec(2), vwt_spec(1), vwt_spec(0),
            pl.BlockSpec((LANES, nc), lambda b, g, i: (0, 0)),
            pl.BlockSpec((3 * NSA_TQ, LANES), lambda b, g, i: (0, 0)),
        ],
        out_specs=pl.BlockSpec((NSA_TQ, gw), lambda b, g, i: (b * nq + i, g)),
        out_shape=jax.ShapeDtypeStruct((bsz * seq, HQ), BF16),
        scratch_shapes=[
            pltpu.VMEM((NSA_COLS, KSX), BF16),
            pltpu.VMEM((1, NSA_COLS), F32),
            pltpu.VMEM((1, NSA_COLS), F32),
            pltpu.VMEM((LANES, NSA_COLS), F32),
            pltpu.VMEM((LANES, NSA_COLS), F32),
            pltpu.VMEM((LANES, NSA_COLS), F32),
            pltpu.VMEM((NSA_TK, NSA_COLS // 2), F32),
            pltpu.VMEM((NSA_TK, NSA_COLS // 2), F32),
        ],
        compiler_params=_cparams(("parallel", "parallel", "arbitrary")),
        name="nsa_attn",
    )(slopes, q, gates, kc, vct, ks, vst, kw, kw, kw, vwt, vwt, vwt, agg_t, _nsa_window_offsets())


def _alibi_slopes(n):
    def pow2(k):
        start = 2.0 ** (-8.0 / k)
        return [start ** (i + 1) for i in range(k)]
    if math.log2(n).is_integer():
        s = pow2(n)
    else:
        c = 2 ** math.floor(math.log2(n))
        s = pow2(c) + pow2(2 * c)[0::2][: n - c]
    return np.asarray(s, np.float32)


def _head_slots(w, width):
    r = w.shape[0]
    w3 = w.reshape(r, TOK_HEADS, width)
    return jnp.pad(w3, ((0, 0), (0, 0), (0, LANES - width))).reshape(r, HQ)


def _rope_pair(w_x1, w_x2):
    half = MLA_ROPE // 2
    r = w_x1.shape[0]
    z_lo = jnp.zeros((r, MLA_NOPE), w_x1.dtype)
    z_hi = jnp.zeros((r, LANES - MLA_NOPE - MLA_ROPE), w_x1.dtype)
    assert w_x1.shape[1] == half
    return (jnp.concatenate([z_lo, w_x1, w_x2, z_hi], 1),
            jnp.concatenate([z_lo, -w_x2, w_x1, z_hi], 1))


def _mla_weights(w_in, w_uq, w_ukv):
    half = MLA_ROPE // 2
    o = MLA_Q_RANK + MLA_KV_RANK
    kr, kr_sw = _rope_pair(w_in[:, o:o + half], w_in[:, o + half:o + MLA_ROPE])
    w_in_all = jnp.concatenate([w_in[:, :o], kr, kr_sw, w_in[:, o + MLA_ROPE:]], 1)
    wq3 = w_uq.reshape(MLA_Q_RANK, TOK_HEADS, MLA_NOPE + MLA_ROPE)
    nope, x1, x2 = wq3[..., :MLA_NOPE], wq3[..., MLA_NOPE:MLA_NOPE + half], wq3[..., MLA_NOPE + half:]
    z = jnp.zeros((MLA_Q_RANK, TOK_HEADS, LANES - MLA_NOPE - MLA_ROPE), w_uq.dtype)
    wq = jnp.concatenate([nope, x1, x2, z], -1).reshape(MLA_Q_RANK, HQ)
    wq_sw = jnp.concatenate([jnp.zeros_like(nope), -x2, x1, z], -1).reshape(MLA_Q_RANK, HQ)
    wkv3 = w_ukv.reshape(MLA_KV_RANK, TOK_HEADS, MLA_NOPE + HEAD_DIM)
    wk = _head_slots(wkv3[..., :MLA_NOPE].reshape(MLA_KV_RANK, -1), MLA_NOPE)
    wvt = wkv3[..., MLA_NOPE:].reshape(MLA_KV_RANK, TOK_WIDTH).T
    return (w_in_all.astype(BF16), jnp.concatenate([wq, wq_sw], 1).astype(BF16),
            wk.astype(BF16), wvt.astype(BF16))


def _rope_tables(seq):
    half = MLA_ROPE // 2
    freq = ROPE_THETA ** (-jnp.arange(half, dtype=F32) / half)
    ang = jnp.arange(seq).astype(F32)[:, None] * freq[None, :]
    cos, sin = jnp.cos(ang), jnp.sin(ang)
    ones = jnp.ones((seq, MLA_NOPE), F32)
    z_hi = jnp.zeros((seq, LANES - MLA_NOPE - MLA_ROPE), F32)
    cos_t = jnp.concatenate([ones, cos, cos, z_hi], 1)
    sin_t = jnp.concatenate([jnp.zeros_like(ones), sin, sin, z_hi], 1)
    return cos_t, sin_t


def _group_slots(w):
    r = w.shape[0]
    w4 = w.reshape(r, NSA_GROUPS, NSA_HPG, HEAD_DIM)
    tiles = []
    for g in range(NSA_GROUPS):
        pad = ((0, 0), (0, 0), (g * HEAD_DIM, LANES - (g + 1) * HEAD_DIM))
        tiles.append(jnp.pad(w4[:, g], pad))
    return jnp.stack(tiles, 1).reshape(r, HQ)


def _bf16_parts(v):
    v = np.asarray(v, np.float32)
    hi = v.astype(BF16).astype(np.float32)
    mid = (v - hi).astype(BF16).astype(np.float32)
    lo = (v - hi - mid).astype(BF16).astype(np.float32)
    return hi, mid, lo


ALIBI_LANE = HEAD_DIM


def _nsa_q_aux():
    row = np.zeros((TOK_HEADS, LANES), np.float32)
    parts = np.stack(_bf16_parts(_alibi_slopes(TOK_HEADS)), 1)
    row[:, ALIBI_LANE:ALIBI_LANE + 3] = parts
    row[:, ALIBI_LANE + 3:ALIBI_LANE + 6] = parts
    return jnp.asarray(row.reshape(1, HQ))


def _nsa_k_aux(seq):
    pos = np.arange(seq)
    rel = pos % NSA_TK
    aux = np.zeros((seq, KSX), np.float32)
    aux[:, ALIBI_LANE:ALIBI_LANE + 3] = (rel // 256 * 256)[:, None]
    aux[:, ALIBI_LANE + 3:ALIBI_LANE + 6] = (rel % 256)[:, None]
    aux[pos, LANES + pos // SEL_LEN] = 1.0
    return jnp.asarray(aux)


def _nsa_kw_aux(seq):
    aux = np.zeros((seq, LANES), np.float32)
    aux[:, ALIBI_LANE + 3:ALIBI_LANE + 6] = (np.arange(seq) % NSA_TQ)[:, None]
    return jnp.asarray(aux)


def _nsa_window_offsets():
    off = np.zeros((3 * NSA_TQ, LANES), np.float32)
    off[:, ALIBI_LANE:ALIBI_LANE + 3] = (np.arange(3 * NSA_TQ) // NSA_TQ * NSA_TQ)[:, None]
    return jnp.asarray(off, BF16)


def _nsa_kc_aux(rows):
    n = np.arange(rows)
    aux = np.zeros((rows, LANES), np.float32)
    aux[:, ALIBI_LANE:ALIBI_LANE + 3] = (n // 256 * 256 * CMP_STRIDE)[:, None]
    aux[:, ALIBI_LANE + 3:ALIBI_LANE + 6] = (n % 256 * CMP_STRIDE)[:, None]
    return jnp.asarray(aux, BF16)


def _nsa_in_weights(w_in):
    scale = HEAD_DIM ** -0.5
    wq = _head_slots(w_in[:, :TOK_WIDTH] * scale, HEAD_DIM)
    ng = NSA_HPG * NSA_BRANCHES
    wg = w_in[:, TOK_WIDTH:TOK_WIDTH + TOK_HEADS * NSA_BRANCHES].reshape(-1, NSA_GROUPS, ng)
    wg = jnp.pad(wg, ((0, 0), (0, 0), (0, LANES - ng))).reshape(-1, NSA_GATE_COLS)
    return jnp.concatenate([wq, wg, w_in[:, TOK_WIDTH + TOK_HEADS * NSA_BRANCHES:]], 1).astype(BF16)


def _nsa_kv_weights(w_kv):
    w5 = w_kv.reshape(D_MODEL, NSA_BRANCHES, 2, NSA_GROUPS * HEAD_DIM)
    cmp_cols = []
    for kv in range(2):
        for g in range(NSA_GROUPS):
            c = w5[:, 0, kv, g * HEAD_DIM:(g + 1) * HEAD_DIM]
            cmp_cols.append(jnp.pad(c, ((0, 0), (0, LANES - HEAD_DIM))))
    group_cols = lambda w, g, width: jnp.pad(w[:, g * HEAD_DIM:(g + 1) * HEAD_DIM], ((0, 0), (0, width - HEAD_DIM)))
    ks_cols = [group_cols(w5[:, 1, 0], g, KSX) for g in range(NSA_GROUPS)]
    kw_cols = [group_cols(w5[:, 2, 0], g, LANES) for g in range(NSA_GROUPS)]
    w = jnp.concatenate(cmp_cols + ks_cols + kw_cols, 1)
    wvt = jnp.concatenate([w5[:, 1, 1], w5[:, 2, 1]], 1).T
    return w.astype(BF16), wvt.astype(BF16)


def _out_weights_nsa(w_tok):
    return _group_slots(w_tok.T).T


def _agg_matrix(seq, rows):
    n_cmp = (seq - CMP_LEN) // CMP_STRIDE + 1
    n_sel = seq // SEL_LEN
    cmp_start = np.arange(n_cmp) * CMP_STRIDE
    sel_start = np.arange(n_sel) * SEL_LEN
    overlap = np.clip(np.minimum(cmp_start[:, None] + CMP_LEN, sel_start[None, :] + SEL_LEN)
                      - np.maximum(cmp_start[:, None], sel_start[None, :]), 0, None)
    agg_t = np.zeros((LANES, rows), np.float32)
    agg_t[:n_sel, :n_cmp] = (overlap / CMP_LEN).T
    return jnp.asarray(agg_t, BF16)


def kernel(x, mem, ln_g, ln_b, ffn_w_gu, ffn_w_down, w_mem_kv, w_out, mla_w_in, mla_q_norm_g, mla_kv_norm_g,
           mla_w_uq, mla_w_ukv, nsa_w_in, nsa_w_kv, cmp_pos, cmp_w1, cmp_b1, cmp_w2):
    bsz, seq, _ = x.shape
    n = bsz * seq
    assert seq % NSA_TK == 0 and seq % MLA_TQ == 0 and seq // SEL_LEN <= LANES
    h = x.reshape(n, D_MODEL)
    cos_t, sin_t = _rope_tables(seq)
    slopes = jnp.asarray(_alibi_slopes(TOK_HEADS))
    rows_c = seq // CMP_STRIDE
    agg_t = _agg_matrix(seq, rows_c)
    ln = lambda layer, k: (ln_g[layer, k][None, :], ln_b[layer, k][None, :])
    shared = None

    for layer in range(DEPTH):
        h = _ffn_ln(h, *_ffn_weights(ffn_w_gu[layer, 0], ffn_w_down[layer, 0]), *ln(layer, 0))
        km, vm = _mem_kv(mem, w_mem_kv[layer].astype(BF16))
        w_tok = w_out[layer, :TOK_WIDTH]
        w_memo = w_out[layer, TOK_WIDTH:].astype(BF16)
        if layer < N_A_LAYERS:
            w_in_all, wq, wk, wvt = _mla_weights(mla_w_in[layer], mla_w_uq[layer], mla_w_ukv[layer])
            q, k, vt, q_mem = _mla_proj(h, w_in_all, mla_q_norm_g[layer][None, :], mla_kv_norm_g[layer][None, :],
                                        wq, wk, wvt, cos_t, sin_t, seq)
            o_tok = _mla_attn(q, k, vt, bsz, seq)
            w_tok = w_tok.astype(BF16)
        else:
            q, gates, q_mem = _nsa_proj(h, _nsa_in_weights(nsa_w_in[layer - N_A_LAYERS]), _nsa_q_aux())
            o_tok = _nsa_attn(slopes, q, gates, *shared, agg_t, bsz, seq)
            w_tok = _out_weights_nsa(w_tok).astype(BF16)
        h = _mix_out(h, o_tok, q_mem, km, vm, w_tok, w_memo, *ln(layer, 1), seq)
        h = _ffn_ln(h, *_ffn_weights(ffn_w_gu[layer, 1], ffn_w_down[layer, 1]), *ln(layer, 2))
        if layer == N_A_LAYERS - 1:
            zc, ks, kw, vst, vwt = _nsa_kv(h, *_nsa_kv_weights(nsa_w_kv), _nsa_k_aux(seq), _nsa_kw_aux(seq), seq)
            z = zc.reshape(4 * bsz, rows_c, CMP_HALF)
            c = _compress(z, cmp_pos.reshape(2, 2, CMP_HALF), cmp_w1.astype(BF16), cmp_b1[:, None, :],
                          cmp_w2.astype(BF16), bsz)
            c = c.reshape(2, NSA_GROUPS, bsz, rows_c, HEAD_DIM)
            kc = jnp.pad(jnp.swapaxes(c[0], 0, 1), ((0, 0), (0, 0), (0, 0), (0, LANES - HEAD_DIM)))
            kc = kc + _nsa_kc_aux(rows_c)
            vct = jnp.swapaxes(jnp.concatenate([c[1, g] for g in range(NSA_GROUPS)], -1), 1, 2)
            shared = (kc, vct, ks, vst, kw, vwt)
    return h.reshape(bsz, seq, D_MODEL)
```

```python
import functools
import math

import numpy as np
import jax
import jax.numpy as jnp
from jax import lax
from jax.experimental import pallas as pl
from jax.experimental.pallas import tpu as pltpu

F32 = jnp.float32
BF16 = jnp.bfloat16

D_MODEL = 1024
DEPTH = 4
N_A_LAYERS = DEPTH // 2

TOK_HEADS = 12
HEAD_DIM = 64
MEM_HEADS = 4
MEM_HEAD_DIM = 64
TOK_WIDTH = TOK_HEADS * HEAD_DIM
MEM_WIDTH = MEM_HEADS * MEM_HEAD_DIM

MLA_Q_RANK = 256
MLA_KV_RANK = 128
MLA_NOPE = 64
MLA_ROPE = 32
ROPE_THETA = 10000.0

NSA_GROUPS = 2
NSA_HPG = TOK_HEADS // NSA_GROUPS
NSA_BRANCHES = 3
CMP_LEN = 32
CMP_STRIDE = 16
CMP_HIDDEN = 256
SEL_LEN = 64
SEL_SHIFT = 6
SEL_TOPK = 16
WINDOW = 512

D_FF = 2816
DN_ALPHA = (2 * DEPTH) ** 0.25
LN_EPS = 1e-5
RMS_EPS = 1e-6
NEG = -1e30
FORCE_BONUS = 1e4
LOG2E = math.log2(math.e)

LANES = 128
VROWS = HEAD_DIM + 16
VMEM_LIMIT = 56 * 1024 * 1024


def _cparams(sem):
    return pltpu.CompilerParams(dimension_semantics=sem, vmem_limit_bytes=VMEM_LIMIT)


def _layer_norm(y, g, b):
    mu = jnp.mean(y, -1, keepdims=True)
    yc = y - mu
    var = jnp.mean(yc * yc, -1, keepdims=True)
    return yc * lax.rsqrt(var + LN_EPS) * g + b


def _rms_norm(x, g):
    return x * lax.rsqrt(jnp.mean(x * x, -1, keepdims=True) + RMS_EPS) * g


def _dot(a, b):
    return jnp.dot(a, b, preferred_element_type=F32)


def _dot_nt(a, b):
    return lax.dot_general(a, b, (((1,), (1,)), ((), ())), preferred_element_type=F32)


FFN_TM = 1024
FFN_TF = 1408


def _ffn_kernel(x_ref, wg_ref, wu_ref, wd_ref, g_ref, b_ref, o_ref, *, nj):
    j = pl.program_id(1)
    gu = _dot(x_ref[...].astype(BF16), jnp.concatenate([wg_ref[...], wu_ref[...]], 1))
    gate = gu[:, :FFN_TF]
    up = gu[:, FFN_TF:]
    h = (gate * jax.nn.sigmoid(gate) * up).astype(BF16)
    part = _dot(h, wd_ref[...])

    @pl.when(j == 0)
    def _():
        o_ref[...] = part

    @pl.when(j > 0)
    def _():
        o_ref[...] += part

    @pl.when(j == nj - 1)
    def _():
        y = DN_ALPHA * x_ref[...] + 0.5 * o_ref[...]
        o_ref[...] = _layer_norm(y, g_ref[...], b_ref[...])


def _ffn_ln(x, w_gu, w_down, g, b):
    n = x.shape[0]
    nj = D_FF // FFN_TF
    return pl.pallas_call(
        functools.partial(_ffn_kernel, nj=nj),
        grid=(n // FFN_TM, nj),
        in_specs=[
            pl.BlockSpec((FFN_TM, D_MODEL), lambda i, j: (i, 0)),
            pl.BlockSpec((D_MODEL, FFN_TF), lambda i, j: (0, j)),
            pl.BlockSpec((D_MODEL, FFN_TF), lambda i, j: (0, j + nj)),
            pl.BlockSpec((FFN_TF, D_MODEL), lambda i, j: (j, 0)),
            pl.BlockSpec((1, D_MODEL), lambda i, j: (0, 0)),
            pl.BlockSpec((1, D_MODEL), lambda i, j: (0, 0)),
        ],
        out_specs=pl.BlockSpec((FFN_TM, D_MODEL), lambda i, j: (i, 0)),
        out_shape=jax.ShapeDtypeStruct((n, D_MODEL), F32),
        compiler_params=_cparams(("parallel", "arbitrary")),
        name="ffn_ln",
    )(x, w_gu, w_gu, w_down, g, b)


def _memkv_kernel(mem_ref, w_ref, km_ref, v_ref):
    kv = _dot(mem_ref[0].astype(BF16), w_ref[...])
    k = kv[:, :MEM_WIDTH]
    v_ref[0] = kv[:, MEM_WIDTH:].astype(BF16)
    lane = lax.broadcasted_iota(jnp.int32, k.shape, 1)
    for h in range(MEM_HEADS):
        in_head = (lane >= h * MEM_HEAD_DIM) & (lane < (h + 1) * MEM_HEAD_DIM)
        km_ref[0, h] = jnp.where(in_head, k, 0.0).astype(BF16)


def _mem_kv(mem, w_mem_kv):
    bsz, m, _ = mem.shape
    return pl.pallas_call(
        _memkv_kernel,
        grid=(bsz,),
        in_specs=[
            pl.BlockSpec((1, m, D_MODEL), lambda b: (b, 0, 0)),
            pl.BlockSpec((D_MODEL, 2 * MEM_WIDTH), lambda b: (0, 0)),
        ],
        out_specs=[
            pl.BlockSpec((1, MEM_HEADS, m, MEM_WIDTH), lambda b: (b, 0, 0, 0)),
            pl.BlockSpec((1, m, MEM_WIDTH), lambda b: (b, 0, 0)),
        ],
        out_shape=[
            jax.ShapeDtypeStruct((bsz, MEM_HEADS, m, MEM_WIDTH), BF16),
            jax.ShapeDtypeStruct((bsz, m, MEM_WIDTH), BF16),
        ],
        compiler_params=_cparams(("parallel",)),
        name="mem_kv",
    )(mem, w_mem_kv)


OUT_TM = 512


def _mix_out_kernel(x_ref, ot_ref, qm_ref, km_ref, vm_ref, wt_ref, wm_ref, g_ref, b_ref, o_ref):
    qm = qm_ref[...]
    vm = vm_ref[0]
    lane = lax.broadcasted_iota(jnp.int32, (qm.shape[0], MEM_WIDTH), 1)
    o_mem = jnp.zeros((qm.shape[0], MEM_WIDTH), F32)
    for h in range(MEM_HEADS):
        s = _dot_nt(qm, km_ref[0, h]) * (MEM_HEAD_DIM ** -0.5)
        m = jnp.max(s, -1, keepdims=True)
        e = jnp.exp(s - m)
        p = (e / jnp.sum(e, -1, keepdims=True)).astype(BF16)
        pv = _dot(p, vm)
        in_head = (lane >= h * MEM_HEAD_DIM) & (lane < (h + 1) * MEM_HEAD_DIM)
        o_mem = jnp.where(in_head, pv, o_mem)
    mix = _dot(ot_ref[...], wt_ref[...]) + _dot(o_mem.astype(BF16), wm_ref[...])
    y = DN_ALPHA * x_ref[...] + mix
    o_ref[...] = _layer_norm(y, g_ref[...], b_ref[...])


def _mix_out(x, o_tok, q_mem, km, vm, w_tok, w_mem, g, b, seq):
    n = x.shape[0]
    kt = o_tok.shape[1]
    m = vm.shape[1]
    per_b = seq // OUT_TM
    return pl.pallas_call(
        _mix_out_kernel,
        grid=(n // OUT_TM,),
        in_specs=[
            pl.BlockSpec((OUT_TM, D_MODEL), lambda i: (i, 0)),
            pl.BlockSpec((OUT_TM, kt), lambda i: (i, 0)),
            pl.BlockSpec((OUT_TM, MEM_WIDTH), lambda i: (i, 0)),
            pl.BlockSpec((1, MEM_HEADS, m, MEM_WIDTH), lambda i: (i // per_b, 0, 0, 0)),
            pl.BlockSpec((1, m, MEM_WIDTH), lambda i: (i // per_b, 0, 0)),
            pl.BlockSpec((kt, D_MODEL), lambda i: (0, 0)),
            pl.BlockSpec((MEM_WIDTH, D_MODEL), lambda i: (0, 0)),
            pl.BlockSpec((1, D_MODEL), lambda i: (0, 0)),
            pl.BlockSpec((1, D_MODEL), lambda i: (0, 0)),
        ],
        out_specs=pl.BlockSpec((OUT_TM, D_MODEL), lambda i: (i, 0)),
        out_shape=jax.ShapeDtypeStruct((n, D_MODEL), F32),
        compiler_params=_cparams(("parallel",)),
        name="mix_out",
    )(x, o_tok, q_mem, km, vm, w_tok, w_mem, g, b)


MLA_TM = 512
MLA_IN_COLS = MLA_Q_RANK + MLA_KV_RANK + 2 * LANES + MEM_WIDTH
HQ = TOK_HEADS * LANES


def _mla_proj_kernel(x_ref, win_ref, qg_ref, kvg_ref, wq_ref, wk_ref, wvt_ref, cos_ref, sin_ref,
                     q_ref, k_ref, vt_ref, qm_ref):
    xb = x_ref[...].astype(BF16)
    hh = _dot(xb, win_ref[...])
    c_q = hh[:, :MLA_Q_RANK]
    c_kv = hh[:, MLA_Q_RANK:MLA_Q_RANK + MLA_KV_RANK]
    o = MLA_Q_RANK + MLA_KV_RANK
    kr = hh[:, o:o + LANES]
    kr_sw = hh[:, o + LANES:o + 2 * LANES]
    qm_ref[...] = hh[:, o + 2 * LANES:].astype(BF16)
    cos = cos_ref[...]
    sin = sin_ref[...]
    qq = _dot(_rms_norm(c_q, qg_ref[...]).astype(BF16), wq_ref[...])
    ckv = _rms_norm(c_kv, kvg_ref[...]).astype(BF16)
    kk = _dot(ckv, wk_ref[...])
    vt_ref[0] = _dot_nt(wvt_ref[...], ckv).astype(BF16)
    kr_rot = kr * cos + kr_sw * sin
    for h in range(TOK_HEADS):
        sl = slice(h * LANES, (h + 1) * LANES)
        sl2 = slice(HQ + h * LANES, HQ + (h + 1) * LANES)
        q_ref[:, sl] = (qq[:, sl] * cos + qq[:, sl2] * sin).astype(BF16)
        k_ref[:, sl] = (kk[:, sl] + kr_rot).astype(BF16)


def _mla_proj(x, w_in, qg, kvg, w_q, w_k, w_vt, cos_t, sin_t, seq):
    n = x.shape[0]
    per_b = seq // MLA_TM
    full = lambda shape: pl.BlockSpec(shape, lambda i: (0, 0))
    return pl.pallas_call(
        _mla_proj_kernel,
        grid=(n // MLA_TM,),
        in_specs=[
            pl.BlockSpec((MLA_TM, D_MODEL), lambda i: (i, 0)),
            full((D_MODEL, MLA_IN_COLS)),
            full((1, MLA_Q_RANK)),
            full((1, MLA_KV_RANK)),
            full((MLA_Q_RANK, 2 * HQ)),
            full((MLA_KV_RANK, HQ)),
            full((TOK_WIDTH, MLA_KV_RANK)),
            pl.BlockSpec((MLA_TM, LANES), lambda i: (i % per_b, 0)),
            pl.BlockSpec((MLA_TM, LANES), lambda i: (i % per_b, 0)),
        ],
        out_specs=[
            pl.BlockSpec((MLA_TM, HQ), lambda i: (i, 0)),
            pl.BlockSpec((MLA_TM, HQ), lambda i: (i, 0)),
            pl.BlockSpec((1, TOK_WIDTH, MLA_TM), lambda i: (i // per_b, 0, i % per_b)),
            pl.BlockSpec((MLA_TM, MEM_WIDTH), lambda i: (i, 0)),
        ],
        out_shape=[
            jax.ShapeDtypeStruct((n, HQ), BF16),
            jax.ShapeDtypeStruct((n, HQ), BF16),
            jax.ShapeDtypeStruct((n // seq, TOK_WIDTH, seq), BF16),
            jax.ShapeDtypeStruct((n, MEM_WIDTH), BF16),
        ],
        compiler_params=_cparams(("parallel",)),
        name="mla_proj",
    )(x, w_in, qg, kvg, w_q, w_k, w_vt, cos_t, sin_t)


MLA_TQ = 512
MLA_TK = 512
MLA_SCALE = (MLA_NOPE + MLA_ROPE) ** -0.5
MLA_EXP2_SCALE = MLA_SCALE * math.log2(math.e)


def _mla_attn_kernel(q_ref, k_ref, vt_ref, o_ref, m_ref, l_ref, acc_ref, s0_ref, s1_ref):
    i = pl.program_id(2)
    m_ref[...] = jnp.full(m_ref.shape, NEG, F32)
    l_ref[...] = jnp.zeros(l_ref.shape, F32)
    acc_ref[...] = jnp.zeros(acc_ref.shape, F32)
    s_refs = (s0_ref, s1_ref)

    def scores(j, hd):
        start = pl.multiple_of(j * MLA_TK, MLA_TK)
        sl = slice(hd * LANES, (hd + 1) * LANES)
        s_refs[hd][...] = _dot_nt(k_ref[pl.ds(start, MLA_TK), sl], q_ref[:, sl])

    def update(j, hd, masked):
        start = pl.multiple_of(j * MLA_TK, MLA_TK)
        rows = slice(hd * HEAD_DIM, (hd + 1) * HEAD_DIM)
        s = s_refs[hd][...]
        if masked:
            krow = lax.broadcasted_iota(jnp.int32, (MLA_TK, MLA_TQ), 0)
            qcol = lax.broadcasted_iota(jnp.int32, (MLA_TK, MLA_TQ), 1)
            s = jnp.where(krow <= qcol, s, NEG)
        m_old = m_ref[hd]
        m_new = jnp.maximum(m_old, jnp.max(s, 0, keepdims=True))
        alpha = jnp.exp2((m_old - m_new) * MLA_EXP2_SCALE)
        p = jnp.exp2((s - m_new) * MLA_EXP2_SCALE)
        l_ref[hd] = alpha * l_ref[hd] + jnp.sum(p, 0, keepdims=True)
        acc_ref[rows, :] = alpha * acc_ref[rows, :] + _dot(vt_ref[0, rows, pl.ds(start, MLA_TK)], p.astype(BF16))
        m_ref[hd] = m_new

    scores(0, 0)

    def tile(j):
        scores(j, 1)
        update(j, 0, False)
        scores(j + 1, 0)
        update(j, 1, False)

    def two_tiles(jj, carry):
        tile(2 * jj)
        tile(2 * jj + 1)
        return carry

    lax.fori_loop(0, i // 2, two_tiles, 0)

    @pl.when(i % 2 == 1)
    def _():
        tile(i - 1)

    scores(i, 1)
    update(i, 0, True)
    update(i, 1, True)
    inv = jnp.concatenate(
        [jnp.broadcast_to(1.0 / jnp.maximum(l_ref[hd], 1e-30), (HEAD_DIM, MLA_TQ)) for hd in range(2)], 0)
    o_ref[...] = (acc_ref[...] * inv).T.astype(BF16)


def _normalized(acc, any_valid=None):
    inv = 1.0 / jnp.maximum(acc[HEAD_DIM:HEAD_DIM + 1, :], 1e-30)
    if any_valid is not None:
        inv = jnp.where(any_valid, inv, 0.0)
    return acc[:HEAD_DIM, :] * inv


def _mla_attn(q, k, vt, bsz, seq):
    assert MLA_TQ == MLA_TK
    nq = seq // MLA_TQ
    return pl.pallas_call(
        _mla_attn_kernel,
        grid=(bsz, TOK_HEADS // 2, nq),
        in_specs=[
            pl.BlockSpec((MLA_TQ, 2 * LANES), lambda b, p, i: (b * nq + i, p)),
            pl.BlockSpec((seq, 2 * LANES), lambda b, p, i: (b, p)),
            pl.BlockSpec((1, 2 * HEAD_DIM, seq), lambda b, p, i: (b, p, 0)),
        ],
        out_specs=pl.BlockSpec((MLA_TQ, LANES), lambda b, p, i: (b * nq + i, p)),
        out_shape=jax.ShapeDtypeStruct((bsz * seq, TOK_WIDTH), BF16),
        scratch_shapes=[
            pltpu.VMEM((2, 1, MLA_TQ), F32),
            pltpu.VMEM((2, 1, MLA_TQ), F32),
            pltpu.VMEM((2 * HEAD_DIM, MLA_TQ), F32),
            pltpu.VMEM((MLA_TK, MLA_TQ), F32),
            pltpu.VMEM((MLA_TK, MLA_TQ), F32),
        ],
        compiler_params=_cparams(("parallel", "parallel", "arbitrary")),
        name="mla_attn",
    )(q, k, vt)


KV_TM = 512
NSA_CMP_COLS = 4 * LANES
KSX = 2 * LANES
NSA_KS_COLS = NSA_GROUPS * KSX
NSA_KW_COLS = NSA_GROUPS * LANES
NSA_KV_COLS = NSA_CMP_COLS + NSA_KS_COLS + NSA_KW_COLS


def _nsa_kv_kernel(x_ref, w_ref, wvt_ref, ones_ref, kaux_ref, kwaux_ref, zc_ref, ks_ref, kw_ref, vst_ref, vwt_ref):
    xb = x_ref[...].astype(BF16)
    y = _dot(xb, w_ref[...])
    for c in range(4):
        zc_ref[c] = y[:, c * LANES:c * LANES + HEAD_DIM]
    o = NSA_CMP_COLS
    kaux = kaux_ref[...]
    kwaux = kwaux_ref[...]
    for g in range(NSA_GROUPS):
        ks_ref[:, g * KSX:(g + 1) * KSX] = (y[:, o + g * KSX:o + (g + 1) * KSX] + kaux).astype(BF16)
        ow = o + NSA_KS_COLS + g * LANES
        kw_ref[:, g * LANES:(g + 1) * LANES] = (y[:, ow:ow + LANES] + kwaux).astype(BF16)
    vt = (_dot_nt(wvt_ref[...], xb) + ones_ref[...]).astype(BF16)
    vst_ref[0] = vt[:NSA_GROUPS * VROWS]
    vwt_ref[0] = vt[NSA_GROUPS * VROWS:]


def _nsa_kv(x, w, wvt, ones_col, kaux, kwaux, seq):
    n = x.shape[0]
    per_b = seq // KV_TM
    vrows = NSA_GROUPS * VROWS
    tile_t = lambda: pl.BlockSpec((1, vrows, KV_TM), lambda i: (i // per_b, 0, i % per_b))
    return pl.pallas_call(
        _nsa_kv_kernel,
        grid=(n // KV_TM,),
        in_specs=[
            pl.BlockSpec((KV_TM, D_MODEL), lambda i: (i, 0)),
            pl.BlockSpec((D_MODEL, NSA_KV_COLS), lambda i: (0, 0)),
            pl.BlockSpec((2 * vrows, D_MODEL), lambda i: (0, 0)),
            pl.BlockSpec((2 * vrows, 1), lambda i: (0, 0)),
            pl.BlockSpec((KV_TM, KSX), lambda i: (i % per_b, 0)),
            pl.BlockSpec((KV_TM, LANES), lambda i: (i % per_b, 0)),
        ],
        out_specs=[
            pl.BlockSpec((4, KV_TM, HEAD_DIM), lambda i: (0, i, 0)),
            pl.BlockSpec((KV_TM, NSA_KS_COLS), lambda i: (i, 0)),
            pl.BlockSpec((KV_TM, NSA_KW_COLS), lambda i: (i, 0)),
            tile_t(),
            tile_t(),
        ],
        out_shape=[
            jax.ShapeDtypeStruct((4, n, HEAD_DIM), F32),
            jax.ShapeDtypeStruct((n, NSA_KS_COLS), BF16),
            jax.ShapeDtypeStruct((n, NSA_KW_COLS), BF16),
            jax.ShapeDtypeStruct((n // seq, vrows, seq), BF16),
            jax.ShapeDtypeStruct((n // seq, vrows, seq), BF16),
        ],
        compiler_params=_cparams(("parallel",)),
        name="nsa_kv",
    )(x, w, wvt, ones_col, kaux, kwaux)


CMP_HALF = CMP_STRIDE * HEAD_DIM


def _compress_kernel(z_ref, pos_ref, w1_ref, b1_ref, w2_ref, o_ref):
    r = z_ref[0]
    rows = r.shape[0]
    lo = _dot((r + pos_ref[0, 0:1, :]).astype(BF16), w1_ref[0, :CMP_HALF, :])
    hi = _dot((r + pos_ref[0, 1:2, :]).astype(BF16), w1_ref[0, CMP_HALF:, :])
    pre = lo + pltpu.roll(hi, rows - 1, 0) + b1_ref[0]
    o_ref[0] = _dot(jax.nn.gelu(pre).astype(BF16), w2_ref[0]).astype(BF16)


def _compress(z, pos, w1, b1, w2, bsz):
    nb, rows, _ = z.shape
    per_kv = NSA_GROUPS * bsz
    return pl.pallas_call(
        _compress_kernel,
        grid=(nb,),
        in_specs=[
            pl.BlockSpec((1, rows, CMP_HALF), lambda i: (i, 0, 0)),
            pl.BlockSpec((1, 2, CMP_HALF), lambda i: (i // per_kv, 0, 0)),
            pl.BlockSpec((1, 2 * CMP_HALF, CMP_HIDDEN), lambda i: (i // per_kv, 0, 0)),
            pl.BlockSpec((1, 1, CMP_HIDDEN), lambda i: (i // per_kv, 0, 0)),
            pl.BlockSpec((1, CMP_HIDDEN, HEAD_DIM), lambda i: (i // per_kv, 0, 0)),
        ],
        out_specs=pl.BlockSpec((1, rows, HEAD_DIM), lambda i: (i, 0, 0)),
        out_shape=jax.ShapeDtypeStruct((nb, rows, HEAD_DIM), BF16),
        compiler_params=_cparams(("parallel",)),
        name="nsa_compress",
    )(z, pos, w1, b1, w2)


NSA_TM = 512
NSA_GATE_COLS = NSA_GROUPS * LANES
NSA_IN_COLS = HQ + NSA_GATE_COLS + MEM_WIDTH


def _nsa_proj_kernel(x_ref, w_ref, qaux_ref, q_ref, gt_ref, qm_ref):
    y = _dot(x_ref[...].astype(BF16), w_ref[...])
    q_ref[...] = (y[:, :HQ] + qaux_ref[...]).astype(BF16)
    gt_ref[...] = jax.nn.sigmoid(y[:, HQ:HQ + NSA_GATE_COLS])
    qm_ref[...] = y[:, HQ + NSA_GATE_COLS:].astype(BF16)


def _nsa_proj(x, w, qaux):
    n = x.shape[0]
    return pl.pallas_call(
        _nsa_proj_kernel,
        grid=(n // NSA_TM,),
        in_specs=[
            pl.BlockSpec((NSA_TM, D_MODEL), lambda i: (i, 0)),
            pl.BlockSpec((D_MODEL, NSA_IN_COLS), lambda i: (0, 0)),
            pl.BlockSpec((1, HQ), lambda i: (0, 0)),
        ],
        out_specs=[
            pl.BlockSpec((NSA_TM, HQ), lambda i: (i, 0)),
            pl.BlockSpec((NSA_TM, NSA_GATE_COLS), lambda i: (i, 0)),
            pl.BlockSpec((NSA_TM, MEM_WIDTH), lambda i: (i, 0)),
        ],
        out_shape=[
            jax.ShapeDtypeStruct((n, HQ), BF16),
            jax.ShapeDtypeStruct((n, NSA_GATE_COLS), F32),
            jax.ShapeDtypeStruct((n, MEM_WIDTH), BF16),
        ],
        compiler_params=_cparams(("parallel",)),
        name="nsa_proj",
    )(x, w, qaux)


NSA_TQ = 256
NSA_TK = 512
NSA_COLS = NSA_HPG * NSA_TQ


def _nsa_attn_kernel(slopes_ref, q_ref, gt_ref, kc_ref, vca_ref, ks_ref, vst_ref,
                     kw0_ref, kw1_ref, kw2_ref, vwt0_ref, vwt1_ref, vwt2_ref, whi_ref, o_ref,
                     qs_ref, m_ref, acc_ref, oc_ref, ow_ref, sa_ref, sb_ref, *, n_cmp):
    g = pl.program_id(1)
    i = pl.program_id(2)
    t0 = i * NSA_TQ
    jmax = (t0 + NSA_TQ - 1) // NSA_TK
    hcols = lambda hh: slice(hh * NSA_TQ, (hh + 1) * NSA_TQ)
    slope = [slopes_ref[g * NSA_HPG + hh] for hh in range(NSA_HPG)]

    for hh in range(NSA_HPG):
        qs_ref[hcols(hh), :LANES] = q_ref[:, hh * LANES:(hh + 1) * LANES]
    qs = qs_ref[:, :LANES]

    def exp_keys(s, bias):
        s = s + bias
        return jnp.exp2(((s - jnp.max(s, 0, keepdims=True)) * LOG2E).astype(BF16))

    nc = kc_ref.shape[2]
    blk_c = lax.broadcasted_iota(jnp.int32, (nc, NSA_TQ), 0)
    tq_c = t0 + lax.broadcasted_iota(jnp.int32, (nc, NSA_TQ), 1)
    keep_c = (blk_c * CMP_STRIDE + (CMP_LEN - 1) <= tq_c) & (blk_c < n_cmp)
    bias_c = jnp.where(keep_c, 0.0, NEG)
    valid_c = t0 + lax.broadcasted_iota(jnp.int32, (1, NSA_TQ), 1) >= CMP_LEN - 1
    s_all = _dot_nt(kc_ref[0, 0], qs)
    et = jnp.concatenate([exp_keys(s_all[:, hcols(hh)], bias_c) for hh in range(NSA_HPG)], 1)
    oc = _dot(vca_ref[0, 0], et)
    oc_ref[...] = oc[:VROWS]
    imp = jnp.zeros((LANES, NSA_TQ), F32)
    for hh in range(NSA_HPG):
        inv = jnp.where(valid_c, 1.0 / jnp.maximum(oc[HEAD_DIM:HEAD_DIM + 1, hcols(hh)], 1e-30), 0.0)
        imp = imp + oc[VROWS:, hcols(hh)] * inv

    blk = lax.broadcasted_iota(jnp.int32, (LANES, NSA_TQ), 0)
    tq_s = t0 + lax.broadcasted_iota(jnp.int32, (LANES, NSA_TQ), 1)
    cur = tq_s >> SEL_SHIFT
    forced = (blk == 0) | (blk == cur) | (blk == cur - 1)
    imp = jnp.where(forced, imp + FORCE_BONUS, imp)
    imp = jnp.where(blk * SEL_LEN <= tq_s, imp, NEG)
    blk_f = blk.astype(F32)
    selb = jnp.full((LANES, NSA_TQ), NEG, F32)
    for _ in range(SEL_TOPK):
        mx = jnp.max(imp, 0, keepdims=True)
        first = jnp.min(jnp.where(imp == mx, blk_f, float(LANES)), 0, keepdims=True)
        hit = blk_f == first
        selb = jnp.where(hit, 0.0, selb)
        imp = jnp.where(hit, -jnp.inf, imp)
    selb_q = selb.T.astype(BF16)
    for hh in range(NSA_HPG):
        qs_ref[hcols(hh), LANES:] = selb_q

    kw = jnp.concatenate([kw0_ref[...], kw1_ref[...], kw2_ref[...]], 0) + whi_ref[...]
    vwt = jnp.concatenate([vwt0_ref[0], vwt1_ref[0], vwt2_ref[0]], 1)
    pos_w = t0 - 2 * NSA_TQ + lax.broadcasted_iota(jnp.int32, (3 * NSA_TQ, NSA_TQ), 0)
    tq_w = t0 + lax.broadcasted_iota(jnp.int32, (3 * NSA_TQ, NSA_TQ), 1)
    dist_w = tq_w - pos_w
    bias_w = jnp.where((dist_w >= 0) & (dist_w < WINDOW) & (pos_w >= 0), 0.0, NEG)
    s_all = _dot_nt(kw, qs)
    et = jnp.concatenate([exp_keys(s_all[:, hcols(hh)], bias_w) for hh in range(NSA_HPG)], 1)
    ow_ref[...] = _dot(vwt, et)

    m_ref[...] = jnp.full(m_ref.shape, NEG, F32)
    acc_ref[...] = jnp.zeros(acc_ref.shape, F32)
    s_refs = (sa_ref, sb_ref)
    half = NSA_HPG // 2
    tq_row = (t0 + lax.broadcasted_iota(jnp.int32, (1, NSA_TQ), 1)).astype(F32)

    def scores(j, st):
        start = pl.multiple_of(j * NSA_TK, NSA_TK)
        q_rows = slice(st * half * NSA_TQ, (st + 1) * half * NSA_TQ)
        s_refs[st][...] = _dot_nt(ks_ref[pl.ds(start, NSA_TK), :], qs_ref[q_rows, :])

    def update(j, st, diag):
        start = pl.multiple_of(j * NSA_TK, NSA_TK)
        if diag:
            pos_k = j * NSA_TK + lax.broadcasted_iota(jnp.int32, (NSA_TK, NSA_TQ), 0)
            tq_k = t0 + lax.broadcasted_iota(jnp.int32, (NSA_TK, NSA_TQ), 1)
            keep = pos_k <= tq_k
        rel = tq_row - (j * NSA_TK).astype(F32)
        p_list = []
        for hl in range(half):
            hh = st * half + hl
            c = hcols(hh)
            s = s_refs[st][:, hcols(hl)]
            if diag:
                s = jnp.where(keep, s, NEG)
            shift = slope[hh] * rel
            m_old = m_ref[:, c]
            m_new = jnp.maximum(m_old, jnp.max(s, 0, keepdims=True) - shift)
            acc_ref[:, c] = jnp.exp(m_old - m_new) * acc_ref[:, c]
            m_ref[:, c] = m_new
            p_list.append(jnp.exp2(((s - (m_new + shift)) * LOG2E).astype(BF16)))
        cs = slice(st * half * NSA_TQ, (st + 1) * half * NSA_TQ)
        acc_ref[:, cs] += _dot(vst_ref[0, :, pl.ds(start, NSA_TK)], jnp.concatenate(p_list, 1))

    scores(0, 0)

    def tile(j):
        scores(j, 1)
        update(j, 0, False)
        scores(j + 1, 0)
        update(j, 1, False)

    def two_tiles(jj, carry):
        tile(2 * jj)
        tile(2 * jj + 1)
        return carry

    lax.fori_loop(0, jmax // 2, two_tiles, 0)

    @pl.when(jmax % 2 == 1)
    def _():
        tile(jmax - 1)

    scores(jmax, 1)
    update(jmax, 0, True)
    update(jmax, 1, True)

    gtt = gt_ref[...].T
    outs = []
    for hh in range(NSA_HPG):
        c = hcols(hh)
        r = hh * NSA_BRANCHES
        outs.append(gtt[r:r + 1] * _normalized(oc_ref[:, c], valid_c)
                    + gtt[r + 1:r + 2] * _normalized(acc_ref[:, c])
                    + gtt[r + 2:r + 3] * _normalized(ow_ref[:, c]))
    for pr in range(NSA_HPG // 2):
        o_ref[:, pr * LANES:(pr + 1) * LANES] = jnp.concatenate(outs[2 * pr:2 * pr + 2], 0).T.astype(BF16)


def _nsa_attn(slopes, q, gates, kc, vca, ks, vst, kw, vwt, bsz, seq):
    assert WINDOW <= 2 * NSA_TQ and NSA_HPG % 2 == 0
    nq = seq // NSA_TQ
    nc = kc.shape[2]
    n_cmp = (seq - CMP_LEN) // CMP_STRIDE + 1
    gw = NSA_HPG * LANES
    ow = NSA_HPG * HEAD_DIM
    kw_spec = lambda d: pl.BlockSpec((NSA_TQ, LANES), lambda b, g, i: (b * nq + jnp.maximum(i - d, 0), g))
    vwt_spec = lambda d: pl.BlockSpec((1, VROWS, NSA_TQ), lambda b, g, i: (b, g, jnp.maximum(i - d, 0)))
    return pl.pallas_call(
        functools.partial(_nsa_attn_kernel, n_cmp=n_cmp),
        grid=(bsz, NSA_GROUPS, nq),
        in_specs=[
            pl.BlockSpec(memory_space=pltpu.SMEM),
            pl.BlockSpec((NSA_TQ, gw), lambda b, g, i: (b * nq + i, g)),
            pl.BlockSpec((NSA_TQ, LANES), lambda b, g, i: (b * nq + i, g)),
            pl.BlockSpec((1, 1, nc, LANES), lambda b, g, i: (b, g, 0, 0)),
            pl.BlockSpec((1, 1, VROWS + LANES, nc), lambda b, g, i: (b, g, 0, 0)),
            pl.BlockSpec((seq, KSX), lambda b, g, i: (b, g)),
            pl.BlockSpec((1, VROWS, seq), lambda b, g, i: (b, g, 0)),
            kw_spec(2), kw_spec(1), kw_spec(0),
            vwt_spec(2), vwt_spec(1), vwt_spec(0),
            pl.BlockSpec((3 * NSA_TQ, LANES), lambda b, g, i: (0, 0)),
        ],
        out_specs=pl.BlockSpec((NSA_TQ, ow), lambda b, g, i: (b * nq + i, g)),
        out_shape=jax.ShapeDtypeStruct((bsz * seq, TOK_WIDTH), BF16),
        scratch_shapes=[
            pltpu.VMEM((NSA_COLS, KSX), BF16),
            pltpu.VMEM((1, NSA_COLS), F32),
            pltpu.VMEM((VROWS, NSA_COLS), F32),
            pltpu.VMEM((VROWS, NSA_COLS), F32),
            pltpu.VMEM((VROWS, NSA_COLS), F32),
            pltpu.VMEM((NSA_TK, NSA_COLS // 2), F32),
            pltpu.VMEM((NSA_TK, NSA_COLS // 2), F32),
        ],
        compiler_params=_cparams(("parallel", "parallel", "arbitrary")),
        name="nsa_attn",
    )(slopes, q, gates, kc, vca, ks, vst, kw, kw, kw, vwt, vwt, vwt, _nsa_window_offsets())


def _alibi_slopes(n):
    def pow2(k):
        start = 2.0 ** (-8.0 / k)
        return [start ** (i + 1) for i in range(k)]
    if math.log2(n).is_integer():
        s = pow2(n)
    else:
        c = 2 ** math.floor(math.log2(n))
        s = pow2(c) + pow2(2 * c)[0::2][: n - c]
    return np.asarray(s, np.float32)


def _head_slots(w, width):
    r = w.shape[0]
    w3 = w.reshape(r, TOK_HEADS, width)
    return jnp.pad(w3, ((0, 0), (0, 0), (0, LANES - width))).reshape(r, HQ)


def _vt_rows(w):
    n, _, r = w.shape
    return jnp.pad(w, ((0, 0), (0, VROWS - HEAD_DIM), (0, 0))).reshape(n * VROWS, r)


def _ones_rows(n):
    col = np.zeros((n, VROWS, 1), np.float32)
    col[:, HEAD_DIM] = 1.0
    return jnp.asarray(col.reshape(n * VROWS, 1))


def _rope_pair(w_x1, w_x2):
    half = MLA_ROPE // 2
    r = w_x1.shape[0]
    z_lo = jnp.zeros((r, MLA_NOPE), w_x1.dtype)
    z_hi = jnp.zeros((r, LANES - MLA_NOPE - MLA_ROPE), w_x1.dtype)
    assert w_x1.shape[1] == half
    return (jnp.concatenate([z_lo, w_x1, w_x2, z_hi], 1),
            jnp.concatenate([z_lo, -w_x2, w_x1, z_hi], 1))


def _mla_weights(w_in, w_uq, w_ukv):
    half = MLA_ROPE // 2
    o = MLA_Q_RANK + MLA_KV_RANK
    kr, kr_sw = _rope_pair(w_in[:, o:o + half], w_in[:, o + half:o + MLA_ROPE])
    w_in_all = jnp.concatenate([w_in[:, :o], kr, kr_sw, w_in[:, o + MLA_ROPE:]], 1)
    wq3 = w_uq.reshape(MLA_Q_RANK, TOK_HEADS, MLA_NOPE + MLA_ROPE)
    nope, x1, x2 = wq3[..., :MLA_NOPE], wq3[..., MLA_NOPE:MLA_NOPE + half], wq3[..., MLA_NOPE + half:]
    z = jnp.zeros((MLA_Q_RANK, TOK_HEADS, LANES - MLA_NOPE - MLA_ROPE), w_uq.dtype)
    wq = jnp.concatenate([nope, x1, x2, z], -1).reshape(MLA_Q_RANK, HQ)
    wq_sw = jnp.concatenate([jnp.zeros_like(nope), -x2, x1, z], -1).reshape(MLA_Q_RANK, HQ)
    wkv3 = w_ukv.reshape(MLA_KV_RANK, TOK_HEADS, MLA_NOPE + HEAD_DIM)
    wk = _head_slots(wkv3[..., :MLA_NOPE].reshape(MLA_KV_RANK, -1), MLA_NOPE)
    wvt = wkv3[..., MLA_NOPE:].reshape(MLA_KV_RANK, TOK_WIDTH).T
    return (w_in_all.astype(BF16), jnp.concatenate([wq, wq_sw], 1).astype(BF16),
            wk.astype(BF16), wvt.astype(BF16))


def _rope_tables(seq):
    half = MLA_ROPE // 2
    freq = ROPE_THETA ** (-jnp.arange(half, dtype=F32) / half)
    ang = jnp.arange(seq).astype(F32)[:, None] * freq[None, :]
    cos, sin = jnp.cos(ang), jnp.sin(ang)
    ones = jnp.ones((seq, MLA_NOPE), F32)
    z_hi = jnp.zeros((seq, LANES - MLA_NOPE - MLA_ROPE), F32)
    cos_t = jnp.concatenate([ones, cos, cos, z_hi], 1)
    sin_t = jnp.concatenate([jnp.zeros_like(ones), sin, sin, z_hi], 1)
    return cos_t, sin_t


def _bf16_parts(v):
    v = np.asarray(v, np.float32)
    hi = v.astype(BF16).astype(np.float32)
    mid = (v - hi).astype(BF16).astype(np.float32)
    lo = (v - hi - mid).astype(BF16).astype(np.float32)
    return hi, mid, lo


ALIBI_LANE = HEAD_DIM


def _nsa_q_aux():
    row = np.zeros((TOK_HEADS, LANES), np.float32)
    parts = np.stack(_bf16_parts(_alibi_slopes(TOK_HEADS)), 1)
    row[:, ALIBI_LANE:ALIBI_LANE + 3] = parts
    row[:, ALIBI_LANE + 3:ALIBI_LANE + 6] = parts
    return jnp.asarray(row.reshape(1, HQ))


def _nsa_k_aux(seq):
    pos = np.arange(seq)
    rel = pos % NSA_TK
    aux = np.zeros((seq, KSX), np.float32)
    aux[:, ALIBI_LANE:ALIBI_LANE + 3] = (rel // 256 * 256)[:, None]
    aux[:, ALIBI_LANE + 3:ALIBI_LANE + 6] = (rel % 256)[:, None]
    aux[pos, LANES + pos // SEL_LEN] = 1.0
    return jnp.asarray(aux)


def _nsa_kw_aux(seq):
    aux = np.zeros((seq, LANES), np.float32)
    aux[:, ALIBI_LANE + 3:ALIBI_LANE + 6] = (np.arange(seq) % NSA_TQ)[:, None]
    return jnp.asarray(aux)


def _nsa_window_offsets():
    off = np.zeros((3 * NSA_TQ, LANES), np.float32)
    off[:, ALIBI_LANE:ALIBI_LANE + 3] = (np.arange(3 * NSA_TQ) // NSA_TQ * NSA_TQ)[:, None]
    return jnp.asarray(off, BF16)


def _nsa_kc_aux(rows):
    n = np.arange(rows)
    aux = np.zeros((rows, LANES), np.float32)
    aux[:, ALIBI_LANE:ALIBI_LANE + 3] = (n // 256 * 256 * CMP_STRIDE)[:, None]
    aux[:, ALIBI_LANE + 3:ALIBI_LANE + 6] = (n % 256 * CMP_STRIDE)[:, None]
    return jnp.asarray(aux, BF16)


def _nsa_in_weights(w_in):
    scale = HEAD_DIM ** -0.5
    wq = _head_slots(w_in[:, :TOK_WIDTH] * scale, HEAD_DIM)
    ng = NSA_HPG * NSA_BRANCHES
    wg = w_in[:, TOK_WIDTH:TOK_WIDTH + TOK_HEADS * NSA_BRANCHES].reshape(-1, NSA_GROUPS, ng)
    wg = jnp.pad(wg, ((0, 0), (0, 0), (0, LANES - ng))).reshape(-1, NSA_GATE_COLS)
    return jnp.concatenate([wq, wg, w_in[:, TOK_WIDTH + TOK_HEADS * NSA_BRANCHES:]], 1).astype(BF16)


def _nsa_kv_weights(w_kv):
    w5 = w_kv.reshape(D_MODEL, NSA_BRANCHES, 2, NSA_GROUPS * HEAD_DIM)
    cmp_cols = []
    for kv in range(2):
        for g in range(NSA_GROUPS):
            c = w5[:, 0, kv, g * HEAD_DIM:(g + 1) * HEAD_DIM]
            cmp_cols.append(jnp.pad(c, ((0, 0), (0, LANES - HEAD_DIM))))
    group_cols = lambda w, g, width: jnp.pad(w[:, g * HEAD_DIM:(g + 1) * HEAD_DIM], ((0, 0), (0, width - HEAD_DIM)))
    ks_cols = [group_cols(w5[:, 1, 0], g, KSX) for g in range(NSA_GROUPS)]
    kw_cols = [group_cols(w5[:, 2, 0], g, LANES) for g in range(NSA_GROUPS)]
    w = jnp.concatenate(cmp_cols + ks_cols + kw_cols, 1)
    wv = jnp.stack([w5[:, 1, 1], w5[:, 2, 1]], 0).reshape(2, D_MODEL, NSA_GROUPS, HEAD_DIM)
    wvt = _vt_rows(jnp.transpose(wv, (0, 2, 3, 1)).reshape(2 * NSA_GROUPS, HEAD_DIM, D_MODEL))
    return w.astype(BF16), wvt.astype(BF16)


def _agg_matrix(seq, rows):
    n_cmp = (seq - CMP_LEN) // CMP_STRIDE + 1
    n_sel = seq // SEL_LEN
    cmp_start = np.arange(n_cmp) * CMP_STRIDE
    sel_start = np.arange(n_sel) * SEL_LEN
    overlap = np.clip(np.minimum(cmp_start[:, None] + CMP_LEN, sel_start[None, :] + SEL_LEN)
                      - np.maximum(cmp_start[:, None], sel_start[None, :]), 0, None)
    agg_t = np.zeros((LANES, rows), np.float32)
    agg_t[:n_sel, :n_cmp] = (overlap / CMP_LEN).T
    return jnp.asarray(agg_t, BF16)


def kernel(x, mem, ln_g, ln_b, ffn_w_gu, ffn_w_down, w_mem_kv, w_out, mla_w_in, mla_q_norm_g, mla_kv_norm_g,
           mla_w_uq, mla_w_ukv, nsa_w_in, nsa_w_kv, cmp_pos, cmp_w1, cmp_b1, cmp_w2):
    bsz, seq, _ = x.shape
    n = bsz * seq
    assert seq % NSA_TK == 0 and seq % MLA_TQ == 0 and seq // SEL_LEN <= LANES
    h = x.reshape(n, D_MODEL)
    cos_t, sin_t = _rope_tables(seq)
    slopes = jnp.asarray(_alibi_slopes(TOK_HEADS))
    rows_c = seq // CMP_STRIDE
    agg_t = _agg_matrix(seq, rows_c)
    ln = lambda layer, k: (ln_g[layer, k][None, :], ln_b[layer, k][None, :])
    shared = None

    for layer in range(DEPTH):
        h = _ffn_ln(h, ffn_w_gu[layer, 0].astype(BF16), ffn_w_down[layer, 0].astype(BF16), *ln(layer, 0))
        km, vm = _mem_kv(mem, w_mem_kv[layer].astype(BF16))
        w_tok = w_out[layer, :TOK_WIDTH].astype(BF16)
        w_memo = w_out[layer, TOK_WIDTH:].astype(BF16)
        if layer < N_A_LAYERS:
            w_in_all, wq, wk, wvt = _mla_weights(mla_w_in[layer], mla_w_uq[layer], mla_w_ukv[layer])
            q, k, vt, q_mem = _mla_proj(h, w_in_all, mla_q_norm_g[layer][None, :], mla_kv_norm_g[layer][None, :],
                                        wq, wk, wvt, cos_t, sin_t, seq)
            o_tok = _mla_attn(q, k, vt, bsz, seq)
        else:
            q, gates, q_mem = _nsa_proj(h, _nsa_in_weights(nsa_w_in[layer - N_A_LAYERS]), _nsa_q_aux())
            o_tok = _nsa_attn(slopes, q, gates, *shared, bsz, seq)
        h = _mix_out(h, o_tok, q_mem, km, vm, w_tok, w_memo, *ln(layer, 1), seq)
        h = _ffn_ln(h, ffn_w_gu[layer, 1].astype(BF16), ffn_w_down[layer, 1].astype(BF16), *ln(layer, 2))
        if layer == N_A_LAYERS - 1:
            zc, ks, kw, vst, vwt = _nsa_kv(h, *_nsa_kv_weights(nsa_w_kv), _ones_rows(2 * NSA_GROUPS),
                                           _nsa_k_aux(seq), _nsa_kw_aux(seq), seq)
            z = zc.reshape(4 * bsz, rows_c, CMP_HALF)
            c = _compress(z, cmp_pos.reshape(2, 2, CMP_HALF), cmp_w1.astype(BF16), cmp_b1[:, None, :],
                          cmp_w2.astype(BF16), bsz)
            c = c.reshape(2, NSA_GROUPS, bsz, rows_c, HEAD_DIM)
            kc = jnp.pad(jnp.swapaxes(c[0], 0, 1), ((0, 0), (0, 0), (0, 0), (0, LANES - HEAD_DIM)))
            kc = kc + _nsa_kc_aux(rows_c)
            vct = jnp.pad(jnp.swapaxes(c[1], 2, 3), ((0, 0), (0, 0), (0, VROWS - HEAD_DIM), (0, 0)))
            vct = vct.at[:, :, HEAD_DIM, :].set(1.0)
            agg_b = jnp.broadcast_to(agg_t, (NSA_GROUPS, bsz) + agg_t.shape)
            vca = jnp.swapaxes(jnp.concatenate([vct, agg_b], 2), 0, 1)
            shared = (kc, vca, ks, vst, kw, vwt)
    return h.reshape(bsz, seq, D_MODEL)
```

```python
import functools
import math

import numpy as np
import jax
import jax.numpy as jnp
from jax import lax
from jax.experimental import pallas as pl
from jax.experimental.pallas import tpu as pltpu

F32 = jnp.float32
BF16 = jnp.bfloat16

D_MODEL = 1024
DEPTH = 4
N_A_LAYERS = DEPTH // 2

TOK_HEADS = 12
HEAD_DIM = 64
MEM_HEADS = 4
MEM_HEAD_DIM = 64
TOK_WIDTH = TOK_HEADS * HEAD_DIM
MEM_WIDTH = MEM_HEADS * MEM_HEAD_DIM

MLA_Q_RANK = 256
MLA_KV_RANK = 128
MLA_NOPE = 64
MLA_ROPE = 32
ROPE_THETA = 10000.0

NSA_GROUPS = 2
NSA_HPG = TOK_HEADS // NSA_GROUPS
NSA_BRANCHES = 3
CMP_LEN = 32
CMP_STRIDE = 16
CMP_HIDDEN = 256
SEL_LEN = 64
SEL_SHIFT = 6
SEL_TOPK = 16
WINDOW = 512

D_FF = 2816
DN_ALPHA = (2 * DEPTH) ** 0.25
LN_EPS = 1e-5
RMS_EPS = 1e-6
NEG = -1e30
FORCE_BONUS = 1e4
LOG2E = math.log2(math.e)

LANES = 128
VROWS = HEAD_DIM + 16
VMEM_LIMIT = 56 * 1024 * 1024


def _cparams(sem):
    return pltpu.CompilerParams(dimension_semantics=sem, vmem_limit_bytes=VMEM_LIMIT)


def _layer_norm(y, g, b):
    mu = jnp.mean(y, -1, keepdims=True)
    yc = y - mu
    var = jnp.mean(yc * yc, -1, keepdims=True)
    return yc * lax.rsqrt(var + LN_EPS) * g + b


def _rms_norm(x, g):
    return x * lax.rsqrt(jnp.mean(x * x, -1, keepdims=True) + RMS_EPS) * g


def _dot(a, b):
    return jnp.dot(a, b, preferred_element_type=F32)


def _dot_nt(a, b):
    return lax.dot_general(a, b, (((1,), (1,)), ((), ())), preferred_element_type=F32)


FFN_TM = 1024
FFN_TF = 1408


def _ffn_kernel(x_ref, wg_ref, wu_ref, wd_ref, g_ref, b_ref, o_ref, *, nj):
    j = pl.program_id(1)
    gu = _dot(x_ref[...].astype(BF16), jnp.concatenate([wg_ref[...], wu_ref[...]], 1))
    gate = gu[:, :FFN_TF]
    up = gu[:, FFN_TF:]
    h = (gate * jax.nn.sigmoid(gate) * up).astype(BF16)
    part = _dot(h, wd_ref[...])

    @pl.when(j == 0)
    def _():
        o_ref[...] = part

    @pl.when(j > 0)
    def _():
        o_ref[...] += part

    @pl.when(j == nj - 1)
    def _():
        y = DN_ALPHA * x_ref[...] + 0.5 * o_ref[...]
        o_ref[...] = _layer_norm(y, g_ref[...], b_ref[...])


def _ffn_ln(x, w_gu, w_down, layer, k, g, b):
    n = x.shape[0]
    nj = D_FF // FFN_TF
    return pl.pallas_call(
        functools.partial(_ffn_kernel, nj=nj),
        grid=(n // FFN_TM, nj),
        in_specs=[
            pl.BlockSpec((FFN_TM, D_MODEL), lambda i, j: (i, 0)),
            pl.BlockSpec((None, None, D_MODEL, FFN_TF), lambda i, j: (layer, k, 0, j)),
            pl.BlockSpec((None, None, D_MODEL, FFN_TF), lambda i, j: (layer, k, 0, j + nj)),
            pl.BlockSpec((None, None, FFN_TF, D_MODEL), lambda i, j: (layer, k, j, 0)),
            pl.BlockSpec((1, D_MODEL), lambda i, j: (0, 0)),
            pl.BlockSpec((1, D_MODEL), lambda i, j: (0, 0)),
        ],
        out_specs=pl.BlockSpec((FFN_TM, D_MODEL), lambda i, j: (i, 0)),
        out_shape=jax.ShapeDtypeStruct((n, D_MODEL), F32),
        compiler_params=_cparams(("parallel", "arbitrary")),
        name="ffn_ln",
    )(x, w_gu, w_gu, w_down, g, b)


def _memkv_kernel(mem_ref, wk_ref, wvt_ref, ones_ref, km_ref, vt_ref):
    mb = mem_ref[0].astype(BF16)
    k = _dot(mb, wk_ref[...]) * (MEM_HEAD_DIM ** -0.5)
    lane = lax.broadcasted_iota(jnp.int32, k.shape, 1)
    for h in range(MEM_HEADS):
        in_head = (lane >= h * MEM_HEAD_DIM) & (lane < (h + 1) * MEM_HEAD_DIM)
        km_ref[0, h] = jnp.where(in_head, k, 0.0).astype(BF16)
    vt_ref[0] = (_dot_nt(wvt_ref[...], mb) + ones_ref[...]).astype(BF16)


def _mem_kv(mem, w_k, w_vt, ones_col):
    bsz, m, _ = mem.shape
    vrows = MEM_HEADS * VROWS
    return pl.pallas_call(
        _memkv_kernel,
        grid=(bsz,),
        in_specs=[
            pl.BlockSpec((1, m, D_MODEL), lambda b: (b, 0, 0)),
            pl.BlockSpec((D_MODEL, MEM_WIDTH), lambda b: (0, 0)),
            pl.BlockSpec((vrows, D_MODEL), lambda b: (0, 0)),
            pl.BlockSpec((vrows, 1), lambda b: (0, 0)),
        ],
        out_specs=[
            pl.BlockSpec((1, MEM_HEADS, m, MEM_WIDTH), lambda b: (b, 0, 0, 0)),
            pl.BlockSpec((1, vrows, m), lambda b: (b, 0, 0)),
        ],
        out_shape=[
            jax.ShapeDtypeStruct((bsz, MEM_HEADS, m, MEM_WIDTH), BF16),
            jax.ShapeDtypeStruct((bsz, vrows, m), BF16),
        ],
        compiler_params=_cparams(("parallel",)),
        name="mem_kv",
    )(mem, w_k, w_vt, ones_col)


OUT_TM = 1024


def _mix_out_kernel(x_ref, ot_ref, qm_ref, km_ref, vt_ref, wt_ref, wm_ref, g_ref, b_ref, o_ref):
    qm = qm_ref[...]
    heads = []
    for h in range(MEM_HEADS):
        s = _dot_nt(km_ref[0, h], qm)
        p = jnp.exp2(((s - jnp.max(s, 0, keepdims=True)) * LOG2E).astype(BF16))
        heads.append(_normalized(_dot(vt_ref[0, h * VROWS:(h + 1) * VROWS, :], p)))
    o_mem = jnp.concatenate(heads, 0).T
    mix = _dot(ot_ref[...], wt_ref[...]) + _dot(o_mem.astype(BF16), wm_ref[...])
    y = DN_ALPHA * x_ref[...] + mix
    o_ref[...] = _layer_norm(y, g_ref[...], b_ref[...])


def _mix_out(x, o_tok, q_mem, km, vmt, w_tok, w_mem, g, b, seq):
    n = x.shape[0]
    kt = o_tok.shape[1]
    m = vmt.shape[2]
    per_b = seq // OUT_TM
    return pl.pallas_call(
        _mix_out_kernel,
        grid=(n // OUT_TM,),
        in_specs=[
            pl.BlockSpec((OUT_TM, D_MODEL), lambda i: (i, 0)),
            pl.BlockSpec((OUT_TM, kt), lambda i: (i, 0)),
            pl.BlockSpec((OUT_TM, MEM_WIDTH), lambda i: (i, 0)),
            pl.BlockSpec((1, MEM_HEADS, m, MEM_WIDTH), lambda i: (i // per_b, 0, 0, 0)),
            pl.BlockSpec((1, MEM_HEADS * VROWS, m), lambda i: (i // per_b, 0, 0)),
            pl.BlockSpec((kt, D_MODEL), lambda i: (0, 0)),
            pl.BlockSpec((MEM_WIDTH, D_MODEL), lambda i: (0, 0)),
            pl.BlockSpec((1, D_MODEL), lambda i: (0, 0)),
            pl.BlockSpec((1, D_MODEL), lambda i: (0, 0)),
        ],
        out_specs=pl.BlockSpec((OUT_TM, D_MODEL), lambda i: (i, 0)),
        out_shape=jax.ShapeDtypeStruct((n, D_MODEL), F32),
        compiler_params=_cparams(("parallel",)),
        name="mix_out",
    )(x, o_tok, q_mem, km, vmt, w_tok, w_mem, g, b)


MLA_TM = 1024
MLA_IN_COLS = MLA_Q_RANK + MLA_KV_RANK + 2 * LANES + MEM_WIDTH
HQ = TOK_HEADS * LANES


def _mla_proj_kernel(x_ref, win_ref, qg_ref, kvg_ref, wq_ref, wk_ref, wvt_ref, cos_ref, sin_ref,
                     q_ref, k_ref, vt_ref, qm_ref):
    xb = x_ref[...].astype(BF16)
    hh = _dot(xb, win_ref[...])
    c_q = hh[:, :MLA_Q_RANK]
    c_kv = hh[:, MLA_Q_RANK:MLA_Q_RANK + MLA_KV_RANK]
    o = MLA_Q_RANK + MLA_KV_RANK
    kr = hh[:, o:o + LANES]
    kr_sw = hh[:, o + LANES:o + 2 * LANES]
    qm_ref[...] = hh[:, o + 2 * LANES:].astype(BF16)
    cos = cos_ref[...]
    sin = sin_ref[...]
    qq = _dot(_rms_norm(c_q, qg_ref[...]).astype(BF16), wq_ref[...])
    ckv = _rms_norm(c_kv, kvg_ref[...]).astype(BF16)
    kk = _dot(ckv, wk_ref[...])
    vt_ref[0] = _dot_nt(wvt_ref[...], ckv).astype(BF16)
    kr_rot = kr * cos + kr_sw * sin
    for h in range(TOK_HEADS):
        sl = slice(h * LANES, (h + 1) * LANES)
        sl2 = slice(HQ + h * LANES, HQ + (h + 1) * LANES)
        q_ref[:, sl] = (qq[:, sl] * cos + qq[:, sl2] * sin).astype(BF16)
        k_ref[:, sl] = (kk[:, sl] + kr_rot).astype(BF16)


def _mla_proj(x, w_in, qg, kvg, w_q, w_k, w_vt, cos_t, sin_t, seq):
    n = x.shape[0]
    per_b = seq // MLA_TM
    full = lambda shape: pl.BlockSpec(shape, lambda i: (0, 0))
    return pl.pallas_call(
        _mla_proj_kernel,
        grid=(n // MLA_TM,),
        in_specs=[
            pl.BlockSpec((MLA_TM, D_MODEL), lambda i: (i, 0)),
            full((D_MODEL, MLA_IN_COLS)),
            full((1, MLA_Q_RANK)),
            full((1, MLA_KV_RANK)),
            full((MLA_Q_RANK, 2 * HQ)),
            full((MLA_KV_RANK, HQ)),
            full((TOK_WIDTH, MLA_KV_RANK)),
            pl.BlockSpec((MLA_TM, LANES), lambda i: (i % per_b, 0)),
            pl.BlockSpec((MLA_TM, LANES), lambda i: (i % per_b, 0)),
        ],
        out_specs=[
            pl.BlockSpec((MLA_TM, HQ), lambda i: (i, 0)),
            pl.BlockSpec((MLA_TM, HQ), lambda i: (i, 0)),
            pl.BlockSpec((1, TOK_WIDTH, MLA_TM), lambda i: (i // per_b, 0, i % per_b)),
            pl.BlockSpec((MLA_TM, MEM_WIDTH), lambda i: (i, 0)),
        ],
        out_shape=[
            jax.ShapeDtypeStruct((n, HQ), BF16),
            jax.ShapeDtypeStruct((n, HQ), BF16),
            jax.ShapeDtypeStruct((n // seq, TOK_WIDTH, seq), BF16),
            jax.ShapeDtypeStruct((n, MEM_WIDTH), BF16),
        ],
        compiler_params=_cparams(("parallel",)),
        name="mla_proj",
    )(x, w_in, qg, kvg, w_q, w_k, w_vt, cos_t, sin_t)


MLA_TQ = 512
MLA_TK = 512
MLA_SCALE = (MLA_NOPE + MLA_ROPE) ** -0.5
MLA_EXP2_SCALE = MLA_SCALE * math.log2(math.e)


def _mla_attn_kernel(q_ref, k_ref, vt_ref, o_ref, m_ref, l_ref, acc_ref, s0_ref, s1_ref):
    i = pl.program_id(2)
    m_ref[...] = jnp.full(m_ref.shape, NEG, F32)
    l_ref[...] = jnp.zeros(l_ref.shape, F32)
    acc_ref[...] = jnp.zeros(acc_ref.shape, F32)
    s_refs = (s0_ref, s1_ref)

    def scores(j, hd):
        start = pl.multiple_of(j * MLA_TK, MLA_TK)
        sl = slice(hd * LANES, (hd + 1) * LANES)
        s_refs[hd][...] = _dot_nt(k_ref[pl.ds(start, MLA_TK), sl], q_ref[:, sl])

    def update(j, hd, masked):
        start = pl.multiple_of(j * MLA_TK, MLA_TK)
        rows = slice(hd * HEAD_DIM, (hd + 1) * HEAD_DIM)
        s = s_refs[hd][...]
        if masked:
            krow = lax.broadcasted_iota(jnp.int32, (MLA_TK, MLA_TQ), 0)
            qcol = lax.broadcasted_iota(jnp.int32, (MLA_TK, MLA_TQ), 1)
            s = jnp.where(krow <= qcol, s, NEG)
        m_old = m_ref[hd]
        m_new = jnp.maximum(m_old, jnp.max(s, 0, keepdims=True))
        alpha = jnp.exp2((m_old - m_new) * MLA_EXP2_SCALE)
        p = jnp.exp2((s - m_new) * MLA_EXP2_SCALE)
        l_ref[hd] = alpha * l_ref[hd] + jnp.sum(p, 0, keepdims=True)
        acc_ref[rows, :] = alpha * acc_ref[rows, :] + _dot(vt_ref[0, rows, pl.ds(start, MLA_TK)], p.astype(BF16))
        m_ref[hd] = m_new

    scores(0, 0)

    def tile(j):
        scores(j, 1)
        update(j, 0, False)
        scores(j + 1, 0)
        update(j, 1, False)

    def two_tiles(jj, carry):
        tile(2 * jj)
        tile(2 * jj + 1)
        return carry

    lax.fori_loop(0, i // 2, two_tiles, 0)

    @pl.when(i % 2 == 1)
    def _():
        tile(i - 1)

    scores(i, 1)
    update(i, 0, True)
    update(i, 1, True)
    inv = jnp.concatenate(
        [jnp.broadcast_to(1.0 / jnp.maximum(l_ref[hd], 1e-30), (HEAD_DIM, MLA_TQ)) for hd in range(2)], 0)
    o_ref[...] = (acc_ref[...] * inv).T.astype(BF16)


def _normalized(acc, any_valid=None):
    inv = 1.0 / jnp.maximum(acc[HEAD_DIM:HEAD_DIM + 1, :], 1e-30)
    if any_valid is not None:
        inv = jnp.where(any_valid, inv, 0.0)
    return acc[:HEAD_DIM, :] * inv


def _mla_attn(q, k, vt, bsz, seq):
    assert MLA_TQ == MLA_TK
    nq = seq // MLA_TQ
    return pl.pallas_call(
        _mla_attn_kernel,
        grid=(bsz, TOK_HEADS // 2, nq),
        in_specs=[
            pl.BlockSpec((MLA_TQ, 2 * LANES), lambda b, p, i: (b * nq + i, p)),
            pl.BlockSpec((seq, 2 * LANES), lambda b, p, i: (b, p)),
            pl.BlockSpec((1, 2 * HEAD_DIM, seq), lambda b, p, i: (b, p, 0)),
        ],
        out_specs=pl.BlockSpec((MLA_TQ, LANES), lambda b, p, i: (b * nq + i, p)),
        out_shape=jax.ShapeDtypeStruct((bsz * seq, TOK_WIDTH), BF16),
        scratch_shapes=[
            pltpu.VMEM((2, 1, MLA_TQ), F32),
            pltpu.VMEM((2, 1, MLA_TQ), F32),
            pltpu.VMEM((2 * HEAD_DIM, MLA_TQ), F32),
            pltpu.VMEM((MLA_TK, MLA_TQ), F32),
            pltpu.VMEM((MLA_TK, MLA_TQ), F32),
        ],
        compiler_params=_cparams(("parallel", "parallel", "arbitrary")),
        name="mla_attn",
    )(q, k, vt)


KV_TM = 1024
NSA_CMP_COLS = 4 * LANES
KSX = 2 * LANES
NSA_KS_COLS = NSA_GROUPS * KSX
NSA_KW_COLS = NSA_GROUPS * LANES
NSA_KV_COLS = NSA_CMP_COLS + NSA_KS_COLS + NSA_KW_COLS


def _nsa_kv_kernel(x_ref, w_ref, wvt_ref, ones_ref, kaux_ref, kwaux_ref, zc_ref, ks_ref, kw_ref, vst_ref, vwt_ref):
    xb = x_ref[...].astype(BF16)
    y = _dot(xb, w_ref[...])
    for c in range(4):
        zc_ref[c] = y[:, c * LANES:c * LANES + HEAD_DIM]
    o = NSA_CMP_COLS
    kaux = kaux_ref[...]
    kwaux = kwaux_ref[...]
    for g in range(NSA_GROUPS):
        ks_ref[:, g * KSX:(g + 1) * KSX] = (y[:, o + g * KSX:o + (g + 1) * KSX] + kaux).astype(BF16)
        ow = o + NSA_KS_COLS + g * LANES
        kw_ref[:, g * LANES:(g + 1) * LANES] = (y[:, ow:ow + LANES] + kwaux).astype(BF16)
    vt = (_dot_nt(wvt_ref[...], xb) + ones_ref[...]).astype(BF16)
    vst_ref[0] = vt[:NSA_GROUPS * VROWS]
    vwt_ref[0] = vt[NSA_GROUPS * VROWS:]


def _nsa_kv(x, w, wvt, ones_col, kaux, kwaux, seq):
    n = x.shape[0]
    per_b = seq // KV_TM
    vrows = NSA_GROUPS * VROWS
    tile_t = lambda: pl.BlockSpec((1, vrows, KV_TM), lambda i: (i // per_b, 0, i % per_b))
    return pl.pallas_call(
        _nsa_kv_kernel,
        grid=(n // KV_TM,),
        in_specs=[
            pl.BlockSpec((KV_TM, D_MODEL), lambda i: (i, 0)),
            pl.BlockSpec((D_MODEL, NSA_KV_COLS), lambda i: (0, 0)),
            pl.BlockSpec((2 * vrows, D_MODEL), lambda i: (0, 0)),
            pl.BlockSpec((2 * vrows, 1), lambda i: (0, 0)),
            pl.BlockSpec((KV_TM, KSX), lambda i: (i % per_b, 0)),
            pl.BlockSpec((KV_TM, LANES), lambda i: (i % per_b, 0)),
        ],
        out_specs=[
            pl.BlockSpec((4, KV_TM, HEAD_DIM), lambda i: (0, i, 0)),
            pl.BlockSpec((KV_TM, NSA_KS_COLS), lambda i: (i, 0)),
            pl.BlockSpec((KV_TM, NSA_KW_COLS), lambda i: (i, 0)),
            tile_t(),
            tile_t(),
        ],
        out_shape=[
            jax.ShapeDtypeStruct((4, n, HEAD_DIM), F32),
            jax.ShapeDtypeStruct((n, NSA_KS_COLS), BF16),
            jax.ShapeDtypeStruct((n, NSA_KW_COLS), BF16),
            jax.ShapeDtypeStruct((n // seq, vrows, seq), BF16),
            jax.ShapeDtypeStruct((n // seq, vrows, seq), BF16),
        ],
        compiler_params=_cparams(("parallel",)),
        name="nsa_kv",
    )(x, w, wvt, ones_col, kaux, kwaux)


CMP_HALF = CMP_STRIDE * HEAD_DIM


def _compress_kernel(z_ref, pos_ref, w1_ref, b1_ref, w2_ref, o_ref):
    r = z_ref[0]
    rows = r.shape[0]
    lo = _dot((r + pos_ref[0, 0:1, :]).astype(BF16), w1_ref[0, :CMP_HALF, :])
    hi = _dot((r + pos_ref[0, 1:2, :]).astype(BF16), w1_ref[0, CMP_HALF:, :])
    pre = lo + pltpu.roll(hi, rows - 1, 0) + b1_ref[0]
    o_ref[0] = _dot(jax.nn.gelu(pre).astype(BF16), w2_ref[0]).astype(BF16)


def _compress(z, pos, w1, b1, w2, bsz):
    nb, rows, _ = z.shape
    per_kv = NSA_GROUPS * bsz
    return pl.pallas_call(
        _compress_kernel,
        grid=(nb,),
        in_specs=[
            pl.BlockSpec((1, rows, CMP_HALF), lambda i: (i, 0, 0)),
            pl.BlockSpec((1, 2, CMP_HALF), lambda i: (i // per_kv, 0, 0)),
            pl.BlockSpec((1, 2 * CMP_HALF, CMP_HIDDEN), lambda i: (i // per_kv, 0, 0)),
            pl.BlockSpec((1, 1, CMP_HIDDEN), lambda i: (i // per_kv, 0, 0)),
            pl.BlockSpec((1, CMP_HIDDEN, HEAD_DIM), lambda i: (i // per_kv, 0, 0)),
        ],
        out_specs=pl.BlockSpec((1, rows, HEAD_DIM), lambda i: (i, 0, 0)),
        out_shape=jax.ShapeDtypeStruct((nb, rows, HEAD_DIM), BF16),
        compiler_params=_cparams(("parallel",)),
        name="nsa_compress",
    )(z, pos, w1, b1, w2)


NSA_TM = 1024
NSA_GATE_COLS = NSA_GROUPS * LANES
NSA_IN_COLS = HQ + NSA_GATE_COLS + MEM_WIDTH


def _nsa_proj_kernel(x_ref, w_ref, qaux_ref, q_ref, gt_ref, qm_ref):
    y = _dot(x_ref[...].astype(BF16), w_ref[...])
    q_ref[...] = (y[:, :HQ] + qaux_ref[...]).astype(BF16)
    gt_ref[...] = jax.nn.sigmoid(y[:, HQ:HQ + NSA_GATE_COLS])
    qm_ref[...] = y[:, HQ + NSA_GATE_COLS:].astype(BF16)


def _nsa_proj(x, w, qaux):
    n = x.shape[0]
    return pl.pallas_call(
        _nsa_proj_kernel,
        grid=(n // NSA_TM,),
        in_specs=[
            pl.BlockSpec((NSA_TM, D_MODEL), lambda i: (i, 0)),
            pl.BlockSpec((D_MODEL, NSA_IN_COLS), lambda i: (0, 0)),
            pl.BlockSpec((1, HQ), lambda i: (0, 0)),
        ],
        out_specs=[
            pl.BlockSpec((NSA_TM, HQ), lambda i: (i, 0)),
            pl.BlockSpec((NSA_TM, NSA_GATE_COLS), lambda i: (i, 0)),
            pl.BlockSpec((NSA_TM, MEM_WIDTH), lambda i: (i, 0)),
        ],
        out_shape=[
            jax.ShapeDtypeStruct((n, HQ), BF16),
            jax.ShapeDtypeStruct((n, NSA_GATE_COLS), F32),
            jax.ShapeDtypeStruct((n, MEM_WIDTH), BF16),
        ],
        compiler_params=_cparams(("parallel",)),
        name="nsa_proj",
    )(x, w, qaux)


NSA_TQ = 256
NSA_TK = 512
NSA_COLS = NSA_HPG * NSA_TQ


def _nsa_attn_kernel(slopes_ref, q_ref, gt_ref, kc_ref, vca_ref, ks_ref, vst_ref,
                     kw0_ref, kw1_ref, kw2_ref, vwt0_ref, vwt1_ref, vwt2_ref, whi_ref, o_ref,
                     qs_ref, m_ref, acc_ref, oc_ref, ow_ref, sa_ref, sb_ref, *, n_cmp):
    g = pl.program_id(1)
    i = pl.program_id(2)
    t0 = i * NSA_TQ
    jmax = (t0 + NSA_TQ - 1) // NSA_TK
    hcols = lambda hh: slice(hh * NSA_TQ, (hh + 1) * NSA_TQ)
    slope = [slopes_ref[g * NSA_HPG + hh] for hh in range(NSA_HPG)]

    for hh in range(NSA_HPG):
        qs_ref[hcols(hh), :LANES] = q_ref[:, hh * LANES:(hh + 1) * LANES]
    qs = qs_ref[:, :LANES]

    def exp_keys(s, bias):
        s = s + bias
        return jnp.exp2(((s - jnp.max(s, 0, keepdims=True)) * LOG2E).astype(BF16))

    nc = kc_ref.shape[2]
    blk_c = lax.broadcasted_iota(jnp.int32, (nc, NSA_TQ), 0)
    tq_c = t0 + lax.broadcasted_iota(jnp.int32, (nc, NSA_TQ), 1)
    keep_c = (blk_c * CMP_STRIDE + (CMP_LEN - 1) <= tq_c) & (blk_c < n_cmp)
    bias_c = jnp.where(keep_c, 0.0, NEG)
    valid_c = t0 + lax.broadcasted_iota(jnp.int32, (1, NSA_TQ), 1) >= CMP_LEN - 1
    s_all = _dot_nt(kc_ref[0, 0], qs)
    et = jnp.concatenate([exp_keys(s_all[:, hcols(hh)], bias_c) for hh in range(NSA_HPG)], 1)
    oc = _dot(vca_ref[0, 0], et)
    oc_ref[...] = oc[:VROWS]
    imp = jnp.zeros((LANES, NSA_TQ), F32)
    for hh in range(NSA_HPG):
        inv = jnp.where(valid_c, 1.0 / jnp.maximum(oc[HEAD_DIM:HEAD_DIM + 1, hcols(hh)], 1e-30), 0.0)
        imp = imp + oc[VROWS:, hcols(hh)] * inv

    blk = lax.broadcasted_iota(jnp.int32, (LANES, NSA_TQ), 0)
    tq_s = t0 + lax.broadcasted_iota(jnp.int32, (LANES, NSA_TQ), 1)
    cur = tq_s >> SEL_SHIFT
    forced = (blk == 0) | (blk == cur) | (blk == cur - 1)
    imp = jnp.where(forced, imp + FORCE_BONUS, imp)
    imp = jnp.where(blk * SEL_LEN <= tq_s, imp, NEG)
    blk_f = blk.astype(F32)
    selb = jnp.full((LANES, NSA_TQ), NEG, F32)
    for _ in range(SEL_TOPK):
        mx = jnp.max(imp, 0, keepdims=True)
        first = jnp.min(jnp.where(imp == mx, blk_f, float(LANES)), 0, keepdims=True)
        hit = blk_f == first
        selb = jnp.where(hit, 0.0, selb)
        imp = jnp.where(hit, -jnp.inf, imp)
    selb_q = selb.T.astype(BF16)
    for hh in range(NSA_HPG):
        qs_ref[hcols(hh), LANES:] = selb_q

    kw = jnp.concatenate([kw0_ref[...], kw1_ref[...], kw2_ref[...]], 0) + whi_ref[...]
    vwt = jnp.concatenate([vwt0_ref[0], vwt1_ref[0], vwt2_ref[0]], 1)
    pos_w = t0 - 2 * NSA_TQ + lax.broadcasted_iota(jnp.int32, (3 * NSA_TQ, NSA_TQ), 0)
    tq_w = t0 + lax.broadcasted_iota(jnp.int32, (3 * NSA_TQ, NSA_TQ), 1)
    dist_w = tq_w - pos_w
    bias_w = jnp.where((dist_w >= 0) & (dist_w < WINDOW) & (pos_w >= 0), 0.0, NEG)
    s_all = _dot_nt(kw, qs)
    et = jnp.concatenate([exp_keys(s_all[:, hcols(hh)], bias_w) for hh in range(NSA_HPG)], 1)
    ow_ref[...] = _dot(vwt, et)

    m_ref[...] = jnp.full(m_ref.shape, NEG, F32)
    acc_ref[...] = jnp.zeros(acc_ref.shape, F32)
    s_refs = (sa_ref, sb_ref)
    half = NSA_HPG // 2
    tq_row = (t0 + lax.broadcasted_iota(jnp.int32, (1, NSA_TQ), 1)).astype(F32)

    def scores(j, st):
        start = pl.multiple_of(j * NSA_TK, NSA_TK)
        q_rows = slice(st * half * NSA_TQ, (st + 1) * half * NSA_TQ)
        s_refs[st][...] = _dot_nt(ks_ref[pl.ds(start, NSA_TK), :], qs_ref[q_rows, :])

    def update(j, st, diag):
        start = pl.multiple_of(j * NSA_TK, NSA_TK)
        if diag:
            pos_k = j * NSA_TK + lax.broadcasted_iota(jnp.int32, (NSA_TK, NSA_TQ), 0)
            tq_k = t0 + lax.broadcasted_iota(jnp.int32, (NSA_TK, NSA_TQ), 1)
            keep = pos_k <= tq_k
        rel = tq_row - (j * NSA_TK).astype(F32)
        p_list = []
        for hl in range(half):
            hh = st * half + hl
            c = hcols(hh)
            s = s_refs[st][:, hcols(hl)]
            if diag:
                s = jnp.where(keep, s, NEG)
            shift = slope[hh] * rel
            m_old = m_ref[:, c]
            m_new = jnp.maximum(m_old, jnp.max(s, 0, keepdims=True) - shift)
            acc_ref[:, c] = jnp.exp(m_old - m_new) * acc_ref[:, c]
            m_ref[:, c] = m_new
            p_list.append(jnp.exp2(((s - (m_new + shift)) * LOG2E).astype(BF16)))
        cs = slice(st * half * NSA_TQ, (st + 1) * half * NSA_TQ)
        acc_ref[:, cs] += _dot(vst_ref[0, :, pl.ds(start, NSA_TK)], jnp.concatenate(p_list, 1))

    scores(0, 0)

    def tile(j):
        scores(j, 1)
        update(j, 0, False)
        scores(j + 1, 0)
        update(j, 1, False)

    def two_tiles(jj, carry):
        tile(2 * jj)
        tile(2 * jj + 1)
        return carry

    lax.fori_loop(0, jmax // 2, two_tiles, 0)

    @pl.when(jmax % 2 == 1)
    def _():
        tile(jmax - 1)

    scores(jmax, 1)
    update(jmax, 0, True)
    update(jmax, 1, True)

    gtt = gt_ref[...].T
    outs = []
    for hh in range(NSA_HPG):
        c = hcols(hh)
        r = hh * NSA_BRANCHES
        outs.append(gtt[r:r + 1] * _normalized(oc_ref[:, c], valid_c)
                    + gtt[r + 1:r + 2] * _normalized(acc_ref[:, c])
                    + gtt[r + 2:r + 3] * _normalized(ow_ref[:, c]))
    for pr in range(NSA_HPG // 2):
        o_ref[:, pr * LANES:(pr + 1) * LANES] = jnp.concatenate(outs[2 * pr:2 * pr + 2], 0).T.astype(BF16)


def _nsa_attn(slopes, q, gates, kc, vca, ks, vst, kw, vwt, bsz, seq):
    assert WINDOW <= 2 * NSA_TQ and NSA_HPG % 2 == 0
    nq = seq // NSA_TQ
    nc = kc.shape[2]
    n_cmp = (seq - CMP_LEN) // CMP_STRIDE + 1
    gw = NSA_HPG * LANES
    ow = NSA_HPG * HEAD_DIM
    kw_spec = lambda d: pl.BlockSpec((NSA_TQ, LANES), lambda b, g, i: (b * nq + jnp.maximum(i - d, 0), g))
    vwt_spec = lambda d: pl.BlockSpec((1, VROWS, NSA_TQ), lambda b, g, i: (b, g, jnp.maximum(i - d, 0)))
    return pl.pallas_call(
        functools.partial(_nsa_attn_kernel, n_cmp=n_cmp),
        grid=(bsz, NSA_GROUPS, nq),
        in_specs=[
            pl.BlockSpec(memory_space=pltpu.SMEM),
            pl.BlockSpec((NSA_TQ, gw), lambda b, g, i: (b * nq + i, g)),
            pl.BlockSpec((NSA_TQ, LANES), lambda b, g, i: (b * nq + i, g)),
            pl.BlockSpec((1, 1, nc, LANES), lambda b, g, i: (b, g, 0, 0)),
            pl.BlockSpec((1, 1, VROWS + LANES, nc), lambda b, g, i: (b, g, 0, 0)),
            pl.BlockSpec((seq, KSX), lambda b, g, i: (b, g)),
            pl.BlockSpec((1, VROWS, seq), lambda b, g, i: (b, g, 0)),
            kw_spec(2), kw_spec(1), kw_spec(0),
            vwt_spec(2), vwt_spec(1), vwt_spec(0),
            pl.BlockSpec((3 * NSA_TQ, LANES), lambda b, g, i: (0, 0)),
        ],
        out_specs=pl.BlockSpec((NSA_TQ, ow), lambda b, g, i: (b * nq + i, g)),
        out_shape=jax.ShapeDtypeStruct((bsz * seq, TOK_WIDTH), BF16),
        scratch_shapes=[
            pltpu.VMEM((NSA_COLS, KSX), BF16),
            pltpu.VMEM((1, NSA_COLS), F32),
            pltpu.VMEM((VROWS, NSA_COLS), F32),
            pltpu.VMEM((VROWS, NSA_COLS), F32),
            pltpu.VMEM((VROWS, NSA_COLS), F32),
            pltpu.VMEM((NSA_TK, NSA_COLS // 2), F32),
            pltpu.VMEM((NSA_TK, NSA_COLS // 2), F32),
        ],
        compiler_params=_cparams(("parallel", "parallel", "arbitrary")),
        name="nsa_attn",
    )(slopes, q, gates, kc, vca, ks, vst, kw, kw, kw, vwt, vwt, vwt, _nsa_window_offsets())


def _alibi_slopes(n):
    def pow2(k):
        start = 2.0 ** (-8.0 / k)
        return [start ** (i + 1) for i in range(k)]
    if math.log2(n).is_integer():
        s = pow2(n)
    else:
        c = 2 ** math.floor(math.log2(n))
        s = pow2(c) + pow2(2 * c)[0::2][: n - c]
    return np.asarray(s, np.float32)


def _head_slots(w, width):
    r = w.shape[0]
    w3 = w.reshape(r, TOK_HEADS, width)
    return jnp.pad(w3, ((0, 0), (0, 0), (0, LANES - width))).reshape(r, HQ)


def _vt_rows(w):
    n, _, r = w.shape
    return jnp.pad(w, ((0, 0), (0, VROWS - HEAD_DIM), (0, 0))).reshape(n * VROWS, r)


def _ones_rows(n):
    col = np.zeros((n, VROWS, 1), np.float32)
    col[:, HEAD_DIM] = 1.0
    return jnp.asarray(col.reshape(n * VROWS, 1))


def _rope_pair(w_x1, w_x2):
    half = MLA_ROPE // 2
    r = w_x1.shape[0]
    z_lo = jnp.zeros((r, MLA_NOPE), w_x1.dtype)
    z_hi = jnp.zeros((r, LANES - MLA_NOPE - MLA_ROPE), w_x1.dtype)
    assert w_x1.shape[1] == half
    return (jnp.concatenate([z_lo, w_x1, w_x2, z_hi], 1),
            jnp.concatenate([z_lo, -w_x2, w_x1, z_hi], 1))


def _mla_weights(w_in, w_uq, w_ukv):
    half = MLA_ROPE // 2
    o = MLA_Q_RANK + MLA_KV_RANK
    kr, kr_sw = _rope_pair(w_in[:, o:o + half], w_in[:, o + half:o + MLA_ROPE])
    w_in_all = jnp.concatenate([w_in[:, :o], kr, kr_sw, w_in[:, o + MLA_ROPE:]], 1)
    wq3 = w_uq.reshape(MLA_Q_RANK, TOK_HEADS, MLA_NOPE + MLA_ROPE)
    nope, x1, x2 = wq3[..., :MLA_NOPE], wq3[..., MLA_NOPE:MLA_NOPE + half], wq3[..., MLA_NOPE + half:]
    z = jnp.zeros((MLA_Q_RANK, TOK_HEADS, LANES - MLA_NOPE - MLA_ROPE), w_uq.dtype)
    wq = jnp.concatenate([nope, x1, x2, z], -1).reshape(MLA_Q_RANK, HQ)
    wq_sw = jnp.concatenate([jnp.zeros_like(nope), -x2, x1, z], -1).reshape(MLA_Q_RANK, HQ)
    wkv3 = w_ukv.reshape(MLA_KV_RANK, TOK_HEADS, MLA_NOPE + HEAD_DIM)
    wk = _head_slots(wkv3[..., :MLA_NOPE].reshape(MLA_KV_RANK, -1), MLA_NOPE)
    wvt = wkv3[..., MLA_NOPE:].reshape(MLA_KV_RANK, TOK_WIDTH).T
    return (w_in_all.astype(BF16), jnp.concatenate([wq, wq_sw], 1).astype(BF16),
            wk.astype(BF16), wvt.astype(BF16))


def _rope_tables(seq):
    half = MLA_ROPE // 2
    freq = ROPE_THETA ** (-jnp.arange(half, dtype=F32) / half)
    ang = jnp.arange(seq).astype(F32)[:, None] * freq[None, :]
    cos, sin = jnp.cos(ang), jnp.sin(ang)
    ones = jnp.ones((seq, MLA_NOPE), F32)
    z_hi = jnp.zeros((seq, LANES - MLA_NOPE - MLA_ROPE), F32)
    cos_t = jnp.concatenate([ones, cos, cos, z_hi], 1)
    sin_t = jnp.concatenate([jnp.zeros_like(ones), sin, sin, z_hi], 1)
    return cos_t, sin_t


def _bf16_parts(v):
    v = np.asarray(v, np.float32)
    hi = v.astype(BF16).astype(np.float32)
    mid = (v - hi).astype(BF16).astype(np.float32)
    lo = (v - hi - mid).astype(BF16).astype(np.float32)
    return hi, mid, lo


ALIBI_LANE = HEAD_DIM


def _nsa_q_aux():
    row = np.zeros((TOK_HEADS, LANES), np.float32)
    parts = np.stack(_bf16_parts(_alibi_slopes(TOK_HEADS)), 1)
    row[:, ALIBI_LANE:ALIBI_LANE + 3] = parts
    row[:, ALIBI_LANE + 3:ALIBI_LANE + 6] = parts
    return jnp.asarray(row.reshape(1, HQ))


def _nsa_k_aux(seq):
    pos = np.arange(seq)
    rel = pos % NSA_TK
    aux = np.zeros((seq, KSX), np.float32)
    aux[:, ALIBI_LANE:ALIBI_LANE + 3] = (rel // 256 * 256)[:, None]
    aux[:, ALIBI_LANE + 3:ALIBI_LANE + 6] = (rel % 256)[:, None]
    aux[pos, LANES + pos // SEL_LEN] = 1.0
    return jnp.asarray(aux)


def _nsa_kw_aux(seq):
    aux = np.zeros((seq, LANES), np.float32)
    aux[:, ALIBI_LANE + 3:ALIBI_LANE + 6] = (np.arange(seq) % NSA_TQ)[:, None]
    return jnp.asarray(aux)


def _nsa_window_offsets():
    off = np.zeros((3 * NSA_TQ, LANES), np.float32)
    off[:, ALIBI_LANE:ALIBI_LANE + 3] = (np.arange(3 * NSA_TQ) // NSA_TQ * NSA_TQ)[:, None]
    return jnp.asarray(off, BF16)


def _nsa_kc_aux(rows):
    n = np.arange(rows)
    aux = np.zeros((rows, LANES), np.float32)
    aux[:, ALIBI_LANE:ALIBI_LANE + 3] = (n // 256 * 256 * CMP_STRIDE)[:, None]
    aux[:, ALIBI_LANE + 3:ALIBI_LANE + 6] = (n % 256 * CMP_STRIDE)[:, None]
    return jnp.asarray(aux, BF16)


def _nsa_in_weights(w_in):
    scale = HEAD_DIM ** -0.5
    wq = _head_slots(w_in[:, :TOK_WIDTH] * scale, HEAD_DIM)
    ng = NSA_HPG * NSA_BRANCHES
    wg = w_in[:, TOK_WIDTH:TOK_WIDTH + TOK_HEADS * NSA_BRANCHES].reshape(-1, NSA_GROUPS, ng)
    wg = jnp.pad(wg, ((0, 0), (0, 0), (0, LANES - ng))).reshape(-1, NSA_GATE_COLS)
    return jnp.concatenate([wq, wg, w_in[:, TOK_WIDTH + TOK_HEADS * NSA_BRANCHES:]], 1).astype(BF16)


def _nsa_kv_weights(w_kv):
    w5 = w_kv.reshape(D_MODEL, NSA_BRANCHES, 2, NSA_GROUPS * HEAD_DIM)
    cmp_cols = []
    for kv in range(2):
        for g in range(NSA_GROUPS):
            c = w5[:, 0, kv, g * HEAD_DIM:(g + 1) * HEAD_DIM]
            cmp_cols.append(jnp.pad(c, ((0, 0), (0, LANES - HEAD_DIM))))
    group_cols = lambda w, g, width: jnp.pad(w[:, g * HEAD_DIM:(g + 1) * HEAD_DIM], ((0, 0), (0, width - HEAD_DIM)))
    ks_cols = [group_cols(w5[:, 1, 0], g, KSX) for g in range(NSA_GROUPS)]
    kw_cols = [group_cols(w5[:, 2, 0], g, LANES) for g in range(NSA_GROUPS)]
    w = jnp.concatenate(cmp_cols + ks_cols + kw_cols, 1)
    wv = jnp.stack([w5[:, 1, 1], w5[:, 2, 1]], 0).reshape(2, D_MODEL, NSA_GROUPS, HEAD_DIM)
    wvt = _vt_rows(jnp.transpose(wv, (0, 2, 3, 1)).reshape(2 * NSA_GROUPS, HEAD_DIM, D_MODEL))
    return w.astype(BF16), wvt.astype(BF16)


def _agg_matrix(seq, rows):
    n_cmp = (seq - CMP_LEN) // CMP_STRIDE + 1
    n_sel = seq // SEL_LEN
    cmp_start = np.arange(n_cmp) * CMP_STRIDE
    sel_start = np.arange(n_sel) * SEL_LEN
    overlap = np.clip(np.minimum(cmp_start[:, None] + CMP_LEN, sel_start[None, :] + SEL_LEN)
                      - np.maximum(cmp_start[:, None], sel_start[None, :]), 0, None)
    agg_t = np.zeros((LANES, rows), np.float32)
    agg_t[:n_sel, :n_cmp] = (overlap / CMP_LEN).T
    return jnp.asarray(agg_t, BF16)


def kernel(x, mem, ln_g, ln_b, ffn_w_gu, ffn_w_down, w_mem_kv, w_out, mla_w_in, mla_q_norm_g, mla_kv_norm_g,
           mla_w_uq, mla_w_ukv, nsa_w_in, nsa_w_kv, cmp_pos, cmp_w1, cmp_b1, cmp_w2):
    bsz, seq, _ = x.shape
    n = bsz * seq
    assert seq % NSA_TK == 0 and seq % MLA_TQ == 0 and seq // SEL_LEN <= LANES
    h = x.reshape(n, D_MODEL)
    cos_t, sin_t = _rope_tables(seq)
    slopes = jnp.asarray(_alibi_slopes(TOK_HEADS))
    rows_c = seq // CMP_STRIDE
    agg_t = _agg_matrix(seq, rows_c)
    ln = lambda layer, k: (ln_g[layer, k][None, :], ln_b[layer, k][None, :])
    w_gu_bf = ffn_w_gu.astype(BF16)
    w_down_bf = ffn_w_down.astype(BF16)
    shared = None

    for layer in range(DEPTH):
        h = _ffn_ln(h, w_gu_bf, w_down_bf, layer, 0, *ln(layer, 0))
        w_mv = w_mem_kv[layer, :, MEM_WIDTH:].reshape(D_MODEL, MEM_HEADS, MEM_HEAD_DIM)
        km, vm = _mem_kv(mem, w_mem_kv[layer, :, :MEM_WIDTH].astype(BF16),
                         _vt_rows(jnp.transpose(w_mv, (1, 2, 0))).astype(BF16), _ones_rows(MEM_HEADS))
        w_tok = w_out[layer, :TOK_WIDTH].astype(BF16)
        w_memo = w_out[layer, TOK_WIDTH:].astype(BF16)
        if layer < N_A_LAYERS:
            w_in_all, wq, wk, wvt = _mla_weights(mla_w_in[layer], mla_w_uq[layer], mla_w_ukv[layer])
            q, k, vt, q_mem = _mla_proj(h, w_in_all, mla_q_norm_g[layer][None, :], mla_kv_norm_g[layer][None, :],
                                        wq, wk, wvt, cos_t, sin_t, seq)
            o_tok = _mla_attn(q, k, vt, bsz, seq)
        else:
            q, gates, q_mem = _nsa_proj(h, _nsa_in_weights(nsa_w_in[layer - N_A_LAYERS]), _nsa_q_aux())
            o_tok = _nsa_attn(slopes, q, gates, *shared, bsz, seq)
        h = _mix_out(h, o_tok, q_mem, km, vm, w_tok, w_memo, *ln(layer, 1), seq)
        h = _ffn_ln(h, w_gu_bf, w_down_bf, layer, 1, *ln(layer, 2))
        if layer == N_A_LAYERS - 1:
            zc, ks, kw, vst, vwt = _nsa_kv(h, *_nsa_kv_weights(nsa_w_kv), _ones_rows(2 * NSA_GROUPS),
                                           _nsa_k_aux(seq), _nsa_kw_aux(seq), seq)
            z = zc.reshape(4 * bsz, rows_c, CMP_HALF)
            c = _compress(z, cmp_pos.reshape(2, 2, CMP_HALF), cmp_w1.astype(BF16), cmp_b1[:, None, :],
                          cmp_w2.astype(BF16), bsz)
            c = c.reshape(2, NSA_GROUPS, bsz, rows_c, HEAD_DIM)
            kc = jnp.pad(jnp.swapaxes(c[0], 0, 1), ((0, 0), (0, 0), (0, 0), (0, LANES - HEAD_DIM)))
            kc = kc + _nsa_kc_aux(rows_c)
            vct = jnp.pad(jnp.swapaxes(c[1], 2, 3), ((0, 0), (0, 0), (0, VROWS - HEAD_DIM), (0, 0)))
            vct = vct.at[:, :, HEAD_DIM, :].set(1.0)
            agg_b = jnp.broadcast_to(agg_t, (NSA_GROUPS, bsz) + agg_t.shape)
            vca = jnp.swapaxes(jnp.concatenate([vct, agg_b], 2), 0, 1)
            shared = (kc, vca, ks, vst, kw, vwt)
    return h.reshape(bsz, seq, D_MODEL)
```

```python
import functools
import math

import numpy as np
import jax
import jax.numpy as jnp
from jax import lax
from jax.experimental import pallas as pl
from jax.experimental.pallas import tpu as pltpu

F32 = jnp.float32
BF16 = jnp.bfloat16

D_MODEL = 1024
DEPTH = 4
N_A_LAYERS = DEPTH // 2

TOK_HEADS = 12
HEAD_DIM = 64
MEM_HEADS = 4
MEM_HEAD_DIM = 64
TOK_WIDTH = TOK_HEADS * HEAD_DIM
MEM_WIDTH = MEM_HEADS * MEM_HEAD_DIM

MLA_Q_RANK = 256
MLA_KV_RANK = 128
MLA_NOPE = 64
MLA_ROPE = 32
ROPE_THETA = 10000.0

NSA_GROUPS = 2
NSA_HPG = TOK_HEADS // NSA_GROUPS
NSA_BRANCHES = 3
CMP_LEN = 32
CMP_STRIDE = 16
CMP_HIDDEN = 256
SEL_LEN = 64
SEL_SHIFT = 6
SEL_TOPK = 16
WINDOW = 512

D_FF = 2816
DN_ALPHA = (2 * DEPTH) ** 0.25
LN_EPS = 1e-5
RMS_EPS = 1e-6
NEG = -1e30
FORCE_BONUS = 1e4
LOG2E = math.log2(math.e)

LANES = 128
VROWS = HEAD_DIM + 16
VMEM_LIMIT = 56 * 1024 * 1024


def _cparams(sem):
    return pltpu.CompilerParams(dimension_semantics=sem, vmem_limit_bytes=VMEM_LIMIT)


def _layer_norm(y, g, b):
    mu = jnp.mean(y, -1, keepdims=True)
    yc = y - mu
    var = jnp.mean(yc * yc, -1, keepdims=True)
    return yc * lax.rsqrt(var + LN_EPS) * g + b


def _rms_norm(x, g):
    return x * lax.rsqrt(jnp.mean(x * x, -1, keepdims=True) + RMS_EPS) * g


def _dot(a, b):
    return jnp.dot(a, b, preferred_element_type=F32)


def _dot_nt(a, b):
    return lax.dot_general(a, b, (((1,), (1,)), ((), ())), preferred_element_type=F32)


FFN_TM = 512


def _ffn_kernel(x_ref, wgu_ref, wd_ref, g_ref, b_ref, o_ref):
    x = x_ref[...]
    gu = _dot(x.astype(BF16), wgu_ref[...])
    gate = gu[:, :D_FF]
    up = gu[:, D_FF:]
    h = (gate * jax.nn.sigmoid(gate) * up).astype(BF16)
    y = DN_ALPHA * x + 0.5 * _dot(h, wd_ref[...])
    o_ref[...] = _layer_norm(y, g_ref[...], b_ref[...])


def _ffn_ln(x, w_gu, w_down, layer, k, g, b):
    n = x.shape[0]
    resident = pl.Buffered(1)
    return pl.pallas_call(
        _ffn_kernel,
        grid=(n // FFN_TM,),
        in_specs=[
            pl.BlockSpec((FFN_TM, D_MODEL), lambda i: (i, 0)),
            pl.BlockSpec((None, None, D_MODEL, 2 * D_FF), lambda i: (layer, k, 0, 0), pipeline_mode=resident),
            pl.BlockSpec((None, None, D_FF, D_MODEL), lambda i: (layer, k, 0, 0), pipeline_mode=resident),
            pl.BlockSpec((1, D_MODEL), lambda i: (0, 0)),
            pl.BlockSpec((1, D_MODEL), lambda i: (0, 0)),
        ],
        out_specs=pl.BlockSpec((FFN_TM, D_MODEL), lambda i: (i, 0)),
        out_shape=jax.ShapeDtypeStruct((n, D_MODEL), F32),
        compiler_params=_cparams(("parallel",)),
        name="ffn_ln",
    )(x, w_gu, w_down, g, b)


def _memkv_kernel(mem_ref, wk_ref, wvt_ref, ones_ref, km_ref, vt_ref):
    mb = mem_ref[0].astype(BF16)
    k = _dot(mb, wk_ref[...]) * (MEM_HEAD_DIM ** -0.5)
    lane = lax.broadcasted_iota(jnp.int32, k.shape, 1)
    for h in range(MEM_HEADS):
        in_head = (lane >= h * MEM_HEAD_DIM) & (lane < (h + 1) * MEM_HEAD_DIM)
        km_ref[0, h] = jnp.where(in_head, k, 0.0).astype(BF16)
    vt_ref[0] = (_dot_nt(wvt_ref[...], mb) + ones_ref[...]).astype(BF16)


def _mem_kv(mem, w_k, w_vt, ones_col):
    bsz, m, _ = mem.shape
    vrows = MEM_HEADS * VROWS
    return pl.pallas_call(
        _memkv_kernel,
        grid=(bsz,),
        in_specs=[
            pl.BlockSpec((1, m, D_MODEL), lambda b: (b, 0, 0)),
            pl.BlockSpec((D_MODEL, MEM_WIDTH), lambda b: (0, 0)),
            pl.BlockSpec((vrows, D_MODEL), lambda b: (0, 0)),
            pl.BlockSpec((vrows, 1), lambda b: (0, 0)),
        ],
        out_specs=[
            pl.BlockSpec((1, MEM_HEADS, m, MEM_WIDTH), lambda b: (b, 0, 0, 0)),
            pl.BlockSpec((1, vrows, m), lambda b: (b, 0, 0)),
        ],
        out_shape=[
            jax.ShapeDtypeStruct((bsz, MEM_HEADS, m, MEM_WIDTH), BF16),
            jax.ShapeDtypeStruct((bsz, vrows, m), BF16),
        ],
        compiler_params=_cparams(("parallel",)),
        name="mem_kv",
    )(mem, w_k, w_vt, ones_col)


OUT_TM = 1024


def _mix_out_kernel(x_ref, ot_ref, qm_ref, km_ref, vt_ref, wt_ref, wm_ref, g_ref, b_ref, o_ref):
    qm = qm_ref[...]
    heads = []
    for h in range(MEM_HEADS):
        s = _dot_nt(km_ref[0, h], qm)
        p = jnp.exp2(((s - jnp.max(s, 0, keepdims=True)) * LOG2E).astype(BF16))
        heads.append(_normalized(_dot(vt_ref[0, h * VROWS:(h + 1) * VROWS, :], p)))
    o_mem = jnp.concatenate(heads, 0).T
    mix = _dot(ot_ref[...], wt_ref[...]) + _dot(o_mem.astype(BF16), wm_ref[...])
    y = DN_ALPHA * x_ref[...] + mix
    o_ref[...] = _layer_norm(y, g_ref[...], b_ref[...])


def _mix_out(x, o_tok, q_mem, km, vmt, w_tok, w_mem, g, b, seq):
    n = x.shape[0]
    kt = o_tok.shape[1]
    m = vmt.shape[2]
    per_b = seq // OUT_TM
    return pl.pallas_call(
        _mix_out_kernel,
        grid=(n // OUT_TM,),
        in_specs=[
            pl.BlockSpec((OUT_TM, D_MODEL), lambda i: (i, 0)),
            pl.BlockSpec((OUT_TM, kt), lambda i: (i, 0)),
            pl.BlockSpec((OUT_TM, MEM_WIDTH), lambda i: (i, 0)),
            pl.BlockSpec((1, MEM_HEADS, m, MEM_WIDTH), lambda i: (i // per_b, 0, 0, 0)),
            pl.BlockSpec((1, MEM_HEADS * VROWS, m), lambda i: (i // per_b, 0, 0)),
            pl.BlockSpec((kt, D_MODEL), lambda i: (0, 0)),
            pl.BlockSpec((MEM_WIDTH, D_MODEL), lambda i: (0, 0)),
            pl.BlockSpec((1, D_MODEL), lambda i: (0, 0)),
            pl.BlockSpec((1, D_MODEL), lambda i: (0, 0)),
        ],
        out_specs=pl.BlockSpec((OUT_TM, D_MODEL), lambda i: (i, 0)),
        out_shape=jax.ShapeDtypeStruct((n, D_MODEL), F32),
        compiler_params=_cparams(("parallel",)),
        name="mix_out",
    )(x, o_tok, q_mem, km, vmt, w_tok, w_mem, g, b)


MLA_TM = 1024
MLA_IN_COLS = MLA_Q_RANK + MLA_KV_RANK + 2 * LANES + MEM_WIDTH
HQ = TOK_HEADS * LANES


def _mla_proj_kernel(x_ref, win_ref, qg_ref, kvg_ref, wq_ref, wk_ref, wvt_ref, cos_ref, sin_ref,
                     q_ref, k_ref, vt_ref, qm_ref):
    xb = x_ref[...].astype(BF16)
    hh = _dot(xb, win_ref[...])
    c_q = hh[:, :MLA_Q_RANK]
    c_kv = hh[:, MLA_Q_RANK:MLA_Q_RANK + MLA_KV_RANK]
    o = MLA_Q_RANK + MLA_KV_RANK
    kr = hh[:, o:o + LANES]
    kr_sw = hh[:, o + LANES:o + 2 * LANES]
    qm_ref[...] = hh[:, o + 2 * LANES:].astype(BF16)
    cos = cos_ref[...]
    sin = sin_ref[...]
    qq = _dot(_rms_norm(c_q, qg_ref[...]).astype(BF16), wq_ref[...])
    ckv = _rms_norm(c_kv, kvg_ref[...]).astype(BF16)
    kk = _dot(ckv, wk_ref[...])
    vt_ref[0] = _dot_nt(wvt_ref[...], ckv).astype(BF16)
    kr_rot = kr * cos + kr_sw * sin
    for h in range(TOK_HEADS):
        sl = slice(h * LANES, (h + 1) * LANES)
        sl2 = slice(HQ + h * LANES, HQ + (h + 1) * LANES)
        q_ref[:, sl] = (qq[:, sl] * cos + qq[:, sl2] * sin).astype(BF16)
        k_ref[:, sl] = (kk[:, sl] + kr_rot).astype(BF16)


def _mla_proj(x, w_in, qg, kvg, w_q, w_k, w_vt, cos_t, sin_t, seq):
    n = x.shape[0]
    per_b = seq // MLA_TM
    full = lambda shape: pl.BlockSpec(shape, lambda i: (0, 0))
    return pl.pallas_call(
        _mla_proj_kernel,
        grid=(n // MLA_TM,),
        in_specs=[
            pl.BlockSpec((MLA_TM, D_MODEL), lambda i: (i, 0)),
            full((D_MODEL, MLA_IN_COLS)),
            full((1, MLA_Q_RANK)),
            full((1, MLA_KV_RANK)),
            full((MLA_Q_RANK, 2 * HQ)),
            full((MLA_KV_RANK, HQ)),
            full((TOK_WIDTH, MLA_KV_RANK)),
            pl.BlockSpec((MLA_TM, LANES), lambda i: (i % per_b, 0)),
            pl.BlockSpec((MLA_TM, LANES), lambda i: (i % per_b, 0)),
        ],
        out_specs=[
            pl.BlockSpec((MLA_TM, HQ), lambda i: (i, 0)),
            pl.BlockSpec((MLA_TM, HQ), lambda i: (i, 0)),
            pl.BlockSpec((1, TOK_WIDTH, MLA_TM), lambda i: (i // per_b, 0, i % per_b)),
            pl.BlockSpec((MLA_TM, MEM_WIDTH), lambda i: (i, 0)),
        ],
        out_shape=[
            jax.ShapeDtypeStruct((n, HQ), BF16),
            jax.ShapeDtypeStruct((n, HQ), BF16),
            jax.ShapeDtypeStruct((n // seq, TOK_WIDTH, seq), BF16),
            jax.ShapeDtypeStruct((n, MEM_WIDTH), BF16),
        ],
        compiler_params=_cparams(("parallel",)),
        name="mla_proj",
    )(x, w_in, qg, kvg, w_q, w_k, w_vt, cos_t, sin_t)


MLA_TQ = 512
MLA_TK = 512
MLA_SCALE = (MLA_NOPE + MLA_ROPE) ** -0.5
MLA_EXP2_SCALE = MLA_SCALE * math.log2(math.e)


def _mla_attn_kernel(q_ref, k_ref, vt_ref, o_ref, m_ref, l_ref, acc_ref, s0_ref, s1_ref):
    i = pl.program_id(2)
    m_ref[...] = jnp.full(m_ref.shape, NEG, F32)
    l_ref[...] = jnp.zeros(l_ref.shape, F32)
    acc_ref[...] = jnp.zeros(acc_ref.shape, F32)
    s_refs = (s0_ref, s1_ref)

    def scores(j, hd):
        start = pl.multiple_of(j * MLA_TK, MLA_TK)
        sl = slice(hd * LANES, (hd + 1) * LANES)
        s_refs[hd][...] = _dot_nt(k_ref[pl.ds(start, MLA_TK), sl], q_ref[:, sl])

    def update(j, hd, masked):
        start = pl.multiple_of(j * MLA_TK, MLA_TK)
        rows = slice(hd * HEAD_DIM, (hd + 1) * HEAD_DIM)
        s = s_refs[hd][...]
        if masked:
            krow = lax.broadcasted_iota(jnp.int32, (MLA_TK, MLA_TQ), 0)
            qcol = lax.broadcasted_iota(jnp.int32, (MLA_TK, MLA_TQ), 1)
            s = jnp.where(krow <= qcol, s, NEG)
        m_old = m_ref[hd]
        m_new = jnp.maximum(m_old, jnp.max(s, 0, keepdims=True))
        alpha = jnp.exp2((m_old - m_new) * MLA_EXP2_SCALE)
        p = jnp.exp2((s - m_new) * MLA_EXP2_SCALE)
        l_ref[hd] = alpha * l_ref[hd] + jnp.sum(p, 0, keepdims=True)
        acc_ref[rows, :] = alpha * acc_ref[rows, :] + _dot(vt_ref[0, rows, pl.ds(start, MLA_TK)], p.astype(BF16))
        m_ref[hd] = m_new

    scores(0, 0)

    def tile(j):
        scores(j, 1)
        update(j, 0, False)
        scores(j + 1, 0)
        update(j, 1, False)

    def two_tiles(jj, carry):
        tile(2 * jj)
        tile(2 * jj + 1)
        return carry

    lax.fori_loop(0, i // 2, two_tiles, 0)

    @pl.when(i % 2 == 1)
    def _():
        tile(i - 1)

    scores(i, 1)
    update(i, 0, True)
    update(i, 1, True)
    inv = jnp.concatenate(
        [jnp.broadcast_to(1.0 / jnp.maximum(l_ref[hd], 1e-30), (HEAD_DIM, MLA_TQ)) for hd in range(2)], 0)
    o_ref[...] = (acc_ref[...] * inv).T.astype(BF16)


def _normalized(acc, any_valid=None):
    inv = 1.0 / jnp.maximum(acc[HEAD_DIM:HEAD_DIM + 1, :], 1e-30)
    if any_valid is not None:
        inv = jnp.where(any_valid, inv, 0.0)
    return acc[:HEAD_DIM, :] * inv


def _mla_attn(q, k, vt, bsz, seq):
    assert MLA_TQ == MLA_TK
    nq = seq // MLA_TQ
    return pl.pallas_call(
        _mla_attn_kernel,
        grid=(bsz, TOK_HEADS // 2, nq),
        in_specs=[
            pl.BlockSpec((MLA_TQ, 2 * LANES), lambda b, p, i: (b * nq + i, p)),
            pl.BlockSpec((seq, 2 * LANES), lambda b, p, i: (b, p)),
            pl.BlockSpec((1, 2 * HEAD_DIM, seq), lambda b, p, i: (b, p, 0)),
        ],
        out_specs=pl.BlockSpec((MLA_TQ, LANES), lambda b, p, i: (b * nq + i, p)),
        out_shape=jax.ShapeDtypeStruct((bsz * seq, TOK_WIDTH), BF16),
        scratch_shapes=[
            pltpu.VMEM((2, 1, MLA_TQ), F32),
            pltpu.VMEM((2, 1, MLA_TQ), F32),
            pltpu.VMEM((2 * HEAD_DIM, MLA_TQ), F32),
            pltpu.VMEM((MLA_TK, MLA_TQ), F32),
            pltpu.VMEM((MLA_TK, MLA_TQ), F32),
        ],
        compiler_params=_cparams(("parallel", "parallel", "arbitrary")),
        name="mla_attn",
    )(q, k, vt)


KV_TM = 1024
NSA_CMP_COLS = 4 * LANES
KSX = 2 * LANES
NSA_KS_COLS = NSA_GROUPS * KSX
NSA_KW_COLS = NSA_GROUPS * LANES
NSA_KV_COLS = NSA_CMP_COLS + NSA_KS_COLS + NSA_KW_COLS


def _nsa_kv_kernel(x_ref, w_ref, wvt_ref, ones_ref, kaux_ref, kwaux_ref, zc_ref, ks_ref, kw_ref, vst_ref, vwt_ref):
    xb = x_ref[...].astype(BF16)
    y = _dot(xb, w_ref[...])
    for c in range(4):
        zc_ref[c] = y[:, c * LANES:c * LANES + HEAD_DIM]
    o = NSA_CMP_COLS
    kaux = kaux_ref[...]
    kwaux = kwaux_ref[...]
    for g in range(NSA_GROUPS):
        ks_ref[:, g * KSX:(g + 1) * KSX] = (y[:, o + g * KSX:o + (g + 1) * KSX] + kaux).astype(BF16)
        ow = o + NSA_KS_COLS + g * LANES
        kw_ref[:, g * LANES:(g + 1) * LANES] = (y[:, ow:ow + LANES] + kwaux).astype(BF16)
    vt = (_dot_nt(wvt_ref[...], xb) + ones_ref[...]).astype(BF16)
    vst_ref[0] = vt[:NSA_GROUPS * VROWS]
    vwt_ref[0] = vt[NSA_GROUPS * VROWS:]


def _nsa_kv(x, w, wvt, ones_col, kaux, kwaux, seq):
    n = x.shape[0]
    per_b = seq // KV_TM
    vrows = NSA_GROUPS * VROWS
    tile_t = lambda: pl.BlockSpec((1, vrows, KV_TM), lambda i: (i // per_b, 0, i % per_b))
    return pl.pallas_call(
        _nsa_kv_kernel,
        grid=(n // KV_TM,),
        in_specs=[
            pl.BlockSpec((KV_TM, D_MODEL), lambda i: (i, 0)),
            pl.BlockSpec((D_MODEL, NSA_KV_COLS), lambda i: (0, 0)),
            pl.BlockSpec((2 * vrows, D_MODEL), lambda i: (0, 0)),
            pl.BlockSpec((2 * vrows, 1), lambda i: (0, 0)),
            pl.BlockSpec((KV_TM, KSX), lambda i: (i % per_b, 0)),
            pl.BlockSpec((KV_TM, LANES), lambda i: (i % per_b, 0)),
        ],
        out_specs=[
            pl.BlockSpec((4, KV_TM, HEAD_DIM), lambda i: (0, i, 0)),
            pl.BlockSpec((KV_TM, NSA_KS_COLS), lambda i: (i, 0)),
            pl.BlockSpec((KV_TM, NSA_KW_COLS), lambda i: (i, 0)),
            tile_t(),
            tile_t(),
        ],
        out_shape=[
            jax.ShapeDtypeStruct((4, n, HEAD_DIM), F32),
            jax.ShapeDtypeStruct((n, NSA_KS_COLS), BF16),
            jax.ShapeDtypeStruct((n, NSA_KW_COLS), BF16),
            jax.ShapeDtypeStruct((n // seq, vrows, seq), BF16),
            jax.ShapeDtypeStruct((n // seq, vrows, seq), BF16),
        ],
        compiler_params=_cparams(("parallel",)),
        name="nsa_kv",
    )(x, w, wvt, ones_col, kaux, kwaux)


CMP_HALF = CMP_STRIDE * HEAD_DIM


def _compress_kernel(z_ref, pos_ref, w1_ref, b1_ref, w2_ref, o_ref):
    r = z_ref[0]
    rows = r.shape[0]
    lo = _dot((r + pos_ref[0, 0:1, :]).astype(BF16), w1_ref[0, :CMP_HALF, :])
    hi = _dot((r + pos_ref[0, 1:2, :]).astype(BF16), w1_ref[0, CMP_HALF:, :])
    pre = lo + pltpu.roll(hi, rows - 1, 0) + b1_ref[0]
    o_ref[0] = _dot(jax.nn.gelu(pre).astype(BF16), w2_ref[0]).astype(BF16)


def _compress(z, pos, w1, b1, w2, bsz):
    nb, rows, _ = z.shape
    per_kv = NSA_GROUPS * bsz
    return pl.pallas_call(
        _compress_kernel,
        grid=(nb,),
        in_specs=[
            pl.BlockSpec((1, rows, CMP_HALF), lambda i: (i, 0, 0)),
            pl.BlockSpec((1, 2, CMP_HALF), lambda i: (i // per_kv, 0, 0)),
            pl.BlockSpec((1, 2 * CMP_HALF, CMP_HIDDEN), lambda i: (i // per_kv, 0, 0)),
            pl.BlockSpec((1, 1, CMP_HIDDEN), lambda i: (i // per_kv, 0, 0)),
            pl.BlockSpec((1, CMP_HIDDEN, HEAD_DIM), lambda i: (i // per_kv, 0, 0)),
        ],
        out_specs=pl.BlockSpec((1, rows, HEAD_DIM), lambda i: (i, 0, 0)),
        out_shape=jax.ShapeDtypeStruct((nb, rows, HEAD_DIM), BF16),
        compiler_params=_cparams(("parallel",)),
        name="nsa_compress",
    )(z, pos, w1, b1, w2)


NSA_TM = 1024
NSA_GATE_COLS = NSA_GROUPS * LANES
NSA_IN_COLS = HQ + NSA_GATE_COLS + MEM_WIDTH


def _nsa_proj_kernel(x_ref, w_ref, qaux_ref, q_ref, gt_ref, qm_ref):
    y = _dot(x_ref[...].astype(BF16), w_ref[...])
    q_ref[...] = (y[:, :HQ] + qaux_ref[...]).astype(BF16)
    gt_ref[...] = jax.nn.sigmoid(y[:, HQ:HQ + NSA_GATE_COLS])
    qm_ref[...] = y[:, HQ + NSA_GATE_COLS:].astype(BF16)


def _nsa_proj(x, w, qaux):
    n = x.shape[0]
    return pl.pallas_call(
        _nsa_proj_kernel,
        grid=(n // NSA_TM,),
        in_specs=[
            pl.BlockSpec((NSA_TM, D_MODEL), lambda i: (i, 0)),
            pl.BlockSpec((D_MODEL, NSA_IN_COLS), lambda i: (0, 0)),
            pl.BlockSpec((1, HQ), lambda i: (0, 0)),
        ],
        out_specs=[
            pl.BlockSpec((NSA_TM, HQ), lambda i: (i, 0)),
            pl.BlockSpec((NSA_TM, NSA_GATE_COLS), lambda i: (i, 0)),
            pl.BlockSpec((NSA_TM, MEM_WIDTH), lambda i: (i, 0)),
        ],
        out_shape=[
            jax.ShapeDtypeStruct((n, HQ), BF16),
            jax.ShapeDtypeStruct((n, NSA_GATE_COLS), F32),
            jax.ShapeDtypeStruct((n, MEM_WIDTH), BF16),
        ],
        compiler_params=_cparams(("parallel",)),
        name="nsa_proj",
    )(x, w, qaux)


NSA_TQ = 256
NSA_TK = 512
NSA_COLS = NSA_HPG * NSA_TQ


def _nsa_attn_kernel(slopes_ref, q_ref, gt_ref, kc_ref, vca_ref, ks_ref, vst_ref,
                     kw0_ref, kw1_ref, kw2_ref, vwt0_ref, vwt1_ref, vwt2_ref, whi_ref, o_ref,
                     qs_ref, m_ref, acc_ref, oc_ref, ow_ref, sa_ref, sb_ref, *, n_cmp):
    g = pl.program_id(1)
    i = pl.program_id(2)
    t0 = i * NSA_TQ
    jmax = (t0 + NSA_TQ - 1) // NSA_TK
    hcols = lambda hh: slice(hh * NSA_TQ, (hh + 1) * NSA_TQ)
    slope = [slopes_ref[g * NSA_HPG + hh] for hh in range(NSA_HPG)]

    for hh in range(NSA_HPG):
        qs_ref[hcols(hh), :LANES] = q_ref[:, hh * LANES:(hh + 1) * LANES]
    qs = qs_ref[:, :LANES]

    def exp_keys(s, bias):
        s = s + bias
        return jnp.exp2(((s - jnp.max(s, 0, keepdims=True)) * LOG2E).astype(BF16))

    nc = kc_ref.shape[2]
    blk_c = lax.broadcasted_iota(jnp.int32, (nc, NSA_TQ), 0)
    tq_c = t0 + lax.broadcasted_iota(jnp.int32, (nc, NSA_TQ), 1)
    keep_c = (blk_c * CMP_STRIDE + (CMP_LEN - 1) <= tq_c) & (blk_c < n_cmp)
    bias_c = jnp.where(keep_c, 0.0, NEG)
    valid_c = t0 + lax.broadcasted_iota(jnp.int32, (1, NSA_TQ), 1) >= CMP_LEN - 1
    s_all = _dot_nt(kc_ref[0, 0], qs)
    et = jnp.concatenate([exp_keys(s_all[:, hcols(hh)], bias_c) for hh in range(NSA_HPG)], 1)
    oc = _dot(vca_ref[0, 0], et)
    oc_ref[...] = oc[:VROWS]
    imp = jnp.zeros((LANES, NSA_TQ), F32)
    for hh in range(NSA_HPG):
        inv = jnp.where(valid_c, 1.0 / jnp.maximum(oc[HEAD_DIM:HEAD_DIM + 1, hcols(hh)], 1e-30), 0.0)
        imp = imp + oc[VROWS:, hcols(hh)] * inv

    blk = lax.broadcasted_iota(jnp.int32, (LANES, NSA_TQ), 0)
    tq_s = t0 + lax.broadcasted_iota(jnp.int32, (LANES, NSA_TQ), 1)
    cur = tq_s >> SEL_SHIFT
    forced = (blk == 0) | (blk == cur) | (blk == cur - 1)
    imp = jnp.where(forced, imp + FORCE_BONUS, imp)
    imp = jnp.where(blk * SEL_LEN <= tq_s, imp, NEG)
    blk_f = blk.astype(F32)
    selb = jnp.full((LANES, NSA_TQ), NEG, F32)
    for _ in range(SEL_TOPK):
        mx = jnp.max(imp, 0, keepdims=True)
        first = jnp.min(jnp.where(imp == mx, blk_f, float(LANES)), 0, keepdims=True)
        hit = blk_f == first
        selb = jnp.where(hit, 0.0, selb)
        imp = jnp.where(hit, -jnp.inf, imp)
    selb_q = selb.T.astype(BF16)
    for hh in range(NSA_HPG):
        qs_ref[hcols(hh), LANES:] = selb_q

    kw = jnp.concatenate([kw0_ref[...], kw1_ref[...], kw2_ref[...]], 0) + whi_ref[...]
    vwt = jnp.concatenate([vwt0_ref[0], vwt1_ref[0], vwt2_ref[0]], 1)
    pos_w = t0 - 2 * NSA_TQ + lax.broadcasted_iota(jnp.int32, (3 * NSA_TQ, NSA_TQ), 0)
    tq_w = t0 + lax.broadcasted_iota(jnp.int32, (3 * NSA_TQ, NSA_TQ), 1)
    dist_w = tq_w - pos_w
    bias_w = jnp.where((dist_w >= 0) & (dist_w < WINDOW) & (pos_w >= 0), 0.0, NEG)
    s_all = _dot_nt(kw, qs)
    et = jnp.concatenate([exp_keys(s_all[:, hcols(hh)], bias_w) for hh in range(NSA_HPG)], 1)
    ow_ref[...] = _dot(vwt, et)

    m_ref[...] = jnp.full(m_ref.shape, NEG, F32)
    acc_ref[...] = jnp.zeros(acc_ref.shape, F32)
    s_refs = (sa_ref, sb_ref)
    half = NSA_HPG // 2
    tq_row = (t0 + lax.broadcasted_iota(jnp.int32, (1, NSA_TQ), 1)).astype(F32)

    def scores(j, st):
        start = pl.multiple_of(j * NSA_TK, NSA_TK)
        q_rows = slice(st * half * NSA_TQ, (st + 1) * half * NSA_TQ)
        s_refs[st][...] = _dot_nt(ks_ref[pl.ds(start, NSA_TK), :], qs_ref[q_rows, :])

    def update(j, st, diag):
        start = pl.multiple_of(j * NSA_TK, NSA_TK)
        if diag:
            pos_k = j * NSA_TK + lax.broadcasted_iota(jnp.int32, (NSA_TK, NSA_TQ), 0)
            tq_k = t0 + lax.broadcasted_iota(jnp.int32, (NSA_TK, NSA_TQ), 1)
            keep = pos_k <= tq_k
        rel = tq_row - (j * NSA_TK).astype(F32)
        p_list = []
        for hl in range(half):
            hh = st * half + hl
            c = hcols(hh)
            s = s_refs[st][:, hcols(hl)]
            if diag:
                s = jnp.where(keep, s, NEG)
            shift = slope[hh] * rel
            m_old = m_ref[:, c]
            m_new = jnp.maximum(m_old, jnp.max(s, 0, keepdims=True) - shift)
            acc_ref[:, c] = jnp.exp(m_old - m_new) * acc_ref[:, c]
            m_ref[:, c] = m_new
            p_list.append(jnp.exp2(((s - (m_new + shift)) * LOG2E).astype(BF16)))
        cs = slice(st * half * NSA_TQ, (st + 1) * half * NSA_TQ)
        acc_ref[:, cs] += _dot(vst_ref[0, :, pl.ds(start, NSA_TK)], jnp.concatenate(p_list, 1))

    scores(0, 0)

    def tile(j):
        scores(j, 1)
        update(j, 0, False)
        scores(j + 1, 0)
        update(j, 1, False)

    def two_tiles(jj, carry):
        tile(2 * jj)
        tile(2 * jj + 1)
        return carry

    lax.fori_loop(0, jmax // 2, two_tiles, 0)

    @pl.when(jmax % 2 == 1)
    def _():
        tile(jmax - 1)

    scores(jmax, 1)
    update(jmax, 0, True)
    update(jmax, 1, True)

    gtt = gt_ref[...].T
    outs = []
    for hh in range(NSA_HPG):
        c = hcols(hh)
        r = hh * NSA_BRANCHES
        outs.append(gtt[r:r + 1] * _normalized(oc_ref[:, c], valid_c)
                    + gtt[r + 1:r + 2] * _normalized(acc_ref[:, c])
                    + gtt[r + 2:r + 3] * _normalized(ow_ref[:, c]))
    for pr in range(NSA_HPG // 2):
        o_ref[:, pr * LANES:(pr + 1) * LANES] = jnp.concatenate(outs[2 * pr:2 * pr + 2], 0).T.astype(BF16)


def _nsa_attn(slopes, q, gates, kc, vca, ks, vst, kw, vwt, bsz, seq):
    assert WINDOW <= 2 * NSA_TQ and NSA_HPG % 2 == 0
    nq = seq // NSA_TQ
    nc = kc.shape[2]
    n_cmp = (seq - CMP_LEN) // CMP_STRIDE + 1
    gw = NSA_HPG * LANES
    ow = NSA_HPG * HEAD_DIM
    kw_spec = lambda d: pl.BlockSpec((NSA_TQ, LANES), lambda b, g, i: (b * nq + jnp.maximum(i - d, 0), g))
    vwt_spec = lambda d: pl.BlockSpec((1, VROWS, NSA_TQ), lambda b, g, i: (b, g, jnp.maximum(i - d, 0)))
    return pl.pallas_call(
        functools.partial(_nsa_attn_kernel, n_cmp=n_cmp),
        grid=(bsz, NSA_GROUPS, nq),
        in_specs=[
            pl.BlockSpec(memory_space=pltpu.SMEM),
            pl.BlockSpec((NSA_TQ, gw), lambda b, g, i: (b * nq + i, g)),
            pl.BlockSpec((NSA_TQ, LANES), lambda b, g, i: (b * nq + i, g)),
            pl.BlockSpec((1, 1, nc, LANES), lambda b, g, i: (b, g, 0, 0)),
            pl.BlockSpec((1, 1, VROWS + LANES, nc), lambda b, g, i: (b, g, 0, 0)),
            pl.BlockSpec((seq, KSX), lambda b, g, i: (b, g)),
            pl.BlockSpec((1, VROWS, seq), lambda b, g, i: (b, g, 0)),
            kw_spec(2), kw_spec(1), kw_spec(0),
            vwt_spec(2), vwt_spec(1), vwt_spec(0),
            pl.BlockSpec((3 * NSA_TQ, LANES), lambda b, g, i: (0, 0)),
        ],
        out_specs=pl.BlockSpec((NSA_TQ, ow), lambda b, g, i: (b * nq + i, g)),
        out_shape=jax.ShapeDtypeStruct((bsz * seq, TOK_WIDTH), BF16),
        scratch_shapes=[
            pltpu.VMEM((NSA_COLS, KSX), BF16),
            pltpu.VMEM((1, NSA_COLS), F32),
            pltpu.VMEM((VROWS, NSA_COLS), F32),
            pltpu.VMEM((VROWS, NSA_COLS), F32),
            pltpu.VMEM((VROWS, NSA_COLS), F32),
            pltpu.VMEM((NSA_TK, NSA_COLS // 2), F32),
            pltpu.VMEM((NSA_TK, NSA_COLS // 2), F32),
        ],
        compiler_params=_cparams(("parallel", "parallel", "arbitrary")),
        name="nsa_attn",
    )(slopes, q, gates, kc, vca, ks, vst, kw, kw, kw, vwt, vwt, vwt, _nsa_window_offsets())


def _alibi_slopes(n):
    def pow2(k):
        start = 2.0 ** (-8.0 / k)
        return [start ** (i + 1) for i in range(k)]
    if math.log2(n).is_integer():
        s = pow2(n)
    else:
        c = 2 ** math.floor(math.log2(n))
        s = pow2(c) + pow2(2 * c)[0::2][: n - c]
    return np.asarray(s, np.float32)


def _head_slots(w, width):
    r = w.shape[0]
    w3 = w.reshape(r, TOK_HEADS, width)
    return jnp.pad(w3, ((0, 0), (0, 0), (0, LANES - width))).reshape(r, HQ)


def _vt_rows(w):
    n, _, r = w.shape
    return jnp.pad(w, ((0, 0), (0, VROWS - HEAD_DIM), (0, 0))).reshape(n * VROWS, r)


def _ones_rows(n):
    col = np.zeros((n, VROWS, 1), np.float32)
    col[:, HEAD_DIM] = 1.0
    return jnp.asarray(col.reshape(n * VROWS, 1))


def _rope_pair(w_x1, w_x2):
    half = MLA_ROPE // 2
    r = w_x1.shape[0]
    z_lo = jnp.zeros((r, MLA_NOPE), w_x1.dtype)
    z_hi = jnp.zeros((r, LANES - MLA_NOPE - MLA_ROPE), w_x1.dtype)
    assert w_x1.shape[1] == half
    return (jnp.concatenate([z_lo, w_x1, w_x2, z_hi], 1),
            jnp.concatenate([z_lo, -w_x2, w_x1, z_hi], 1))


def _mla_weights(w_in, w_uq, w_ukv):
    half = MLA_ROPE // 2
    o = MLA_Q_RANK + MLA_KV_RANK
    kr, kr_sw = _rope_pair(w_in[:, o:o + half], w_in[:, o + half:o + MLA_ROPE])
    w_in_all = jnp.concatenate([w_in[:, :o], kr, kr_sw, w_in[:, o + MLA_ROPE:]], 1)
    wq3 = w_uq.reshape(MLA_Q_RANK, TOK_HEADS, MLA_NOPE + MLA_ROPE)
    nope, x1, x2 = wq3[..., :MLA_NOPE], wq3[..., MLA_NOPE:MLA_NOPE + half], wq3[..., MLA_NOPE + half:]
    z = jnp.zeros((MLA_Q_RANK, TOK_HEADS, LANES - MLA_NOPE - MLA_ROPE), w_uq.dtype)
    wq = jnp.concatenate([nope, x1, x2, z], -1).reshape(MLA_Q_RANK, HQ)
    wq_sw = jnp.concatenate([jnp.zeros_like(nope), -x2, x1, z], -1).reshape(MLA_Q_RANK, HQ)
    wkv3 = w_ukv.reshape(MLA_KV_RANK, TOK_HEADS, MLA_NOPE + HEAD_DIM)
    wk = _head_slots(wkv3[..., :MLA_NOPE].reshape(MLA_KV_RANK, -1), MLA_NOPE)
    wvt = wkv3[..., MLA_NOPE:].reshape(MLA_KV_RANK, TOK_WIDTH).T
    return (w_in_all.astype(BF16), jnp.concatenate([wq, wq_sw], 1).astype(BF16),
            wk.astype(BF16), wvt.astype(BF16))


def _rope_tables(seq):
    half = MLA_ROPE // 2
    freq = ROPE_THETA ** (-jnp.arange(half, dtype=F32) / half)
    ang = jnp.arange(seq).astype(F32)[:, None] * freq[None, :]
    cos, sin = jnp.cos(ang), jnp.sin(ang)
    ones = jnp.ones((seq, MLA_NOPE), F32)
    z_hi = jnp.zeros((seq, LANES - MLA_NOPE - MLA_ROPE), F32)
    cos_t = jnp.concatenate([ones, cos, cos, z_hi], 1)
    sin_t = jnp.concatenate([jnp.zeros_like(ones), sin, sin, z_hi], 1)
    return cos_t, sin_t


def _bf16_parts(v):
    v = np.asarray(v, np.float32)
    hi = v.astype(BF16).astype(np.float32)
    mid = (v - hi).astype(BF16).astype(np.float32)
    lo = (v - hi - mid).astype(BF16).astype(np.float32)
    return hi, mid, lo


ALIBI_LANE = HEAD_DIM


def _nsa_q_aux():
    row = np.zeros((TOK_HEADS, LANES), np.float32)
    parts = np.stack(_bf16_parts(_alibi_slopes(TOK_HEADS)), 1)
    row[:, ALIBI_LANE:ALIBI_LANE + 3] = parts
    row[:, ALIBI_LANE + 3:ALIBI_LANE + 6] = parts
    return jnp.asarray(row.reshape(1, HQ))


def _nsa_k_aux(seq):
    pos = np.arange(seq)
    rel = pos % NSA_TK
    aux = np.zeros((seq, KSX), np.float32)
    aux[:, ALIBI_LANE:ALIBI_LANE + 3] = (rel // 256 * 256)[:, None]
    aux[:, ALIBI_LANE + 3:ALIBI_LANE + 6] = (rel % 256)[:, None]
    aux[pos, LANES + pos // SEL_LEN] = 1.0
    return jnp.asarray(aux)


def _nsa_kw_aux(seq):
    aux = np.zeros((seq, LANES), np.float32)
    aux[:, ALIBI_LANE + 3:ALIBI_LANE + 6] = (np.arange(seq) % NSA_TQ)[:, None]
    return jnp.asarray(aux)


def _nsa_window_offsets():
    off = np.zeros((3 * NSA_TQ, LANES), np.float32)
    off[:, ALIBI_LANE:ALIBI_LANE + 3] = (np.arange(3 * NSA_TQ) // NSA_TQ * NSA_TQ)[:, None]
    return jnp.asarray(off, BF16)


def _nsa_kc_aux(rows):
    n = np.arange(rows)
    aux = np.zeros((rows, LANES), np.float32)
    aux[:, ALIBI_LANE:ALIBI_LANE + 3] = (n // 256 * 256 * CMP_STRIDE)[:, None]
    aux[:, ALIBI_LANE + 3:ALIBI_LANE + 6] = (n % 256 * CMP_STRIDE)[:, None]
    return jnp.asarray(aux, BF16)


def _nsa_in_weights(w_in):
    scale = HEAD_DIM ** -0.5
    wq = _head_slots(w_in[:, :TOK_WIDTH] * scale, HEAD_DIM)
    ng = NSA_HPG * NSA_BRANCHES
    wg = w_in[:, TOK_WIDTH:TOK_WIDTH + TOK_HEADS * NSA_BRANCHES].reshape(-1, NSA_GROUPS, ng)
    wg = jnp.pad(wg, ((0, 0), (0, 0), (0, LANES - ng))).reshape(-1, NSA_GATE_COLS)
    return jnp.concatenate([wq, wg, w_in[:, TOK_WIDTH + TOK_HEADS * NSA_BRANCHES:]], 1).astype(BF16)


def _nsa_kv_weights(w_kv):
    w5 = w_kv.reshape(D_MODEL, NSA_BRANCHES, 2, NSA_GROUPS * HEAD_DIM)
    cmp_cols = []
    for kv in range(2):
        for g in range(NSA_GROUPS):
            c = w5[:, 0, kv, g * HEAD_DIM:(g + 1) * HEAD_DIM]
            cmp_cols.append(jnp.pad(c, ((0, 0), (0, LANES - HEAD_DIM))))
    group_cols = lambda w, g, width: jnp.pad(w[:, g * HEAD_DIM:(g + 1) * HEAD_DIM], ((0, 0), (0, width - HEAD_DIM)))
    ks_cols = [group_cols(w5[:, 1, 0], g, KSX) for g in range(NSA_GROUPS)]
    kw_cols = [group_cols(w5[:, 2, 0], g, LANES) for g in range(NSA_GROUPS)]
    w = jnp.concatenate(cmp_cols + ks_cols + kw_cols, 1)
    wv = jnp.stack([w5[:, 1, 1], w5[:, 2, 1]], 0).reshape(2, D_MODEL, NSA_GROUPS, HEAD_DIM)
    wvt = _vt_rows(jnp.transpose(wv, (0, 2, 3, 1)).reshape(2 * NSA_GROUPS, HEAD_DIM, D_MODEL))
    return w.astype(BF16), wvt.astype(BF16)


def _agg_matrix(seq, rows):
    n_cmp = (seq - CMP_LEN) // CMP_STRIDE + 1
    n_sel = seq // SEL_LEN
    cmp_start = np.arange(n_cmp) * CMP_STRIDE
    sel_start = np.arange(n_sel) * SEL_LEN
    overlap = np.clip(np.minimum(cmp_start[:, None] + CMP_LEN, sel_start[None, :] + SEL_LEN)
                      - np.maximum(cmp_start[:, None], sel_start[None, :]), 0, None)
    agg_t = np.zeros((LANES, rows), np.float32)
    agg_t[:n_sel, :n_cmp] = (overlap / CMP_LEN).T
    return jnp.asarray(agg_t, BF16)


def kernel(x, mem, ln_g, ln_b, ffn_w_gu, ffn_w_down, w_mem_kv, w_out, mla_w_in, mla_q_norm_g, mla_kv_norm_g,
           mla_w_uq, mla_w_ukv, nsa_w_in, nsa_w_kv, cmp_pos, cmp_w1, cmp_b1, cmp_w2):
    bsz, seq, _ = x.shape
    n = bsz * seq
    assert seq % NSA_TK == 0 and seq % MLA_TQ == 0 and seq // SEL_LEN <= LANES
    h = x.reshape(n, D_MODEL)
    cos_t, sin_t = _rope_tables(seq)
    slopes = jnp.asarray(_alibi_slopes(TOK_HEADS))
    rows_c = seq // CMP_STRIDE
    agg_t = _agg_matrix(seq, rows_c)
    ln = lambda layer, k: (ln_g[layer, k][None, :], ln_b[layer, k][None, :])
    w_gu_bf = ffn_w_gu.astype(BF16)
    w_down_bf = ffn_w_down.astype(BF16)
    shared = None

    for layer in range(DEPTH):
        h = _ffn_ln(h, w_gu_bf, w_down_bf, layer, 0, *ln(layer, 0))
        w_mv = w_mem_kv[layer, :, MEM_WIDTH:].reshape(D_MODEL, MEM_HEADS, MEM_HEAD_DIM)
        km, vm = _mem_kv(mem, w_mem_kv[layer, :, :MEM_WIDTH].astype(BF16),
                         _vt_rows(jnp.transpose(w_mv, (1, 2, 0))).astype(BF16), _ones_rows(MEM_HEADS))
        w_tok = w_out[layer, :TOK_WIDTH].astype(BF16)
        w_memo = w_out[layer, TOK_WIDTH:].astype(BF16)
        if layer < N_A_LAYERS:
            w_in_all, wq, wk, wvt = _mla_weights(mla_w_in[layer], mla_w_uq[layer], mla_w_ukv[layer])
            q, k, vt, q_mem = _mla_proj(h, w_in_all, mla_q_norm_g[layer][None, :], mla_kv_norm_g[layer][None, :],
                                        wq, wk, wvt, cos_t, sin_t, seq)
            o_tok = _mla_attn(q, k, vt, bsz, seq)
        else:
            q, gates, q_mem = _nsa_proj(h, _nsa_in_weights(nsa_w_in[layer - N_A_LAYERS]), _nsa_q_aux())
            o_tok = _nsa_attn(slopes, q, gates, *shared, bsz, seq)
        h = _mix_out(h, o_tok, q_mem, km, vm, w_tok, w_memo, *ln(layer, 1), seq)
        h = _ffn_ln(h, w_gu_bf, w_down_bf, layer, 1, *ln(layer, 2))
        if layer == N_A_LAYERS - 1:
            zc, ks, kw, vst, vwt = _nsa_kv(h, *_nsa_kv_weights(nsa_w_kv), _ones_rows(2 * NSA_GROUPS),
                                           _nsa_k_aux(seq), _nsa_kw_aux(seq), seq)
            z = zc.reshape(4 * bsz, rows_c, CMP_HALF)
            c = _compress(z, cmp_pos.reshape(2, 2, CMP_HALF), cmp_w1.astype(BF16), cmp_b1[:, None, :],
                          cmp_w2.astype(BF16), bsz)
            c = c.reshape(2, NSA_GROUPS, bsz, rows_c, HEAD_DIM)
            kc = jnp.pad(jnp.swapaxes(c[0], 0, 1), ((0, 0), (0, 0), (0, 0), (0, LANES - HEAD_DIM)))
            kc = kc + _nsa_kc_aux(rows_c)
            vct = jnp.pad(jnp.swapaxes(c[1], 2, 3), ((0, 0), (0, 0), (0, VROWS - HEAD_DIM), (0, 0)))
            vct = vct.at[:, :, HEAD_DIM, :].set(1.0)
            agg_b = jnp.broadcast_to(agg_t, (NSA_GROUPS, bsz) + agg_t.shape)
            vca = jnp.swapaxes(jnp.concatenate([vct, agg_b], 2), 0, 1)
            shared = (kc, vca, ks, vst, kw, vwt)
    return h.reshape(bsz, seq, D_MODEL)
```

```python
import functools
import math

import numpy as np
import jax
import jax.numpy as jnp
from jax import lax
from jax.experimental import pallas as pl
from jax.experimental.pallas import tpu as pltpu

F32 = jnp.float32
BF16 = jnp.bfloat16

D_MODEL = 1024
DEPTH = 4
N_A_LAYERS = DEPTH // 2

TOK_HEADS = 12
HEAD_DIM = 64
MEM_HEADS = 4
MEM_HEAD_DIM = 64
TOK_WIDTH = TOK_HEADS * HEAD_DIM
MEM_WIDTH = MEM_HEADS * MEM_HEAD_DIM

MLA_Q_RANK = 256
MLA_KV_RANK = 128
MLA_NOPE = 64
MLA_ROPE = 32
ROPE_THETA = 10000.0

NSA_GROUPS = 2
NSA_HPG = TOK_HEADS // NSA_GROUPS
NSA_BRANCHES = 3
CMP_LEN = 32
CMP_STRIDE = 16
CMP_HIDDEN = 256
SEL_LEN = 64
SEL_SHIFT = 6
SEL_TOPK = 16
WINDOW = 512

D_FF = 2816
DN_ALPHA = (2 * DEPTH) ** 0.25
LN_EPS = 1e-5
RMS_EPS = 1e-6
NEG = -1e30
FORCE_BONUS = 1e4
LOG2E = math.log2(math.e)

LANES = 128
VROWS = HEAD_DIM + 16
VMEM_LIMIT = 56 * 1024 * 1024


def _cparams(sem):
    return pltpu.CompilerParams(dimension_semantics=sem, vmem_limit_bytes=VMEM_LIMIT)


def _layer_norm(y, g, b):
    mu = jnp.mean(y, -1, keepdims=True)
    yc = y - mu
    var = jnp.mean(yc * yc, -1, keepdims=True)
    return yc * lax.rsqrt(var + LN_EPS) * g + b


def _rms_norm(x, g):
    return x * lax.rsqrt(jnp.mean(x * x, -1, keepdims=True) + RMS_EPS) * g


def _dot(a, b):
    return jnp.dot(a, b, preferred_element_type=F32)


def _dot_nt(a, b):
    return lax.dot_general(a, b, (((1,), (1,)), ((), ())), preferred_element_type=F32)


FFN_TM = 512


def _ffn_kernel(x_ref, wgu_ref, wd_ref, g_ref, b_ref, o_ref):
    x = x_ref[...]
    gu = _dot(x.astype(BF16), wgu_ref[...])
    gate = gu[:, :D_FF]
    up = gu[:, D_FF:]
    h = (gate * jax.nn.sigmoid(gate) * up).astype(BF16)
    y = DN_ALPHA * x + 0.5 * _dot(h, wd_ref[...])
    o_ref[...] = _layer_norm(y, g_ref[...], b_ref[...])


def _ffn_ln(x, w_gu, w_down, layer, k, g, b):
    n = x.shape[0]
    resident = pl.Buffered(1)
    return pl.pallas_call(
        _ffn_kernel,
        grid=(n // FFN_TM,),
        in_specs=[
            pl.BlockSpec((FFN_TM, D_MODEL), lambda i: (i, 0)),
            pl.BlockSpec((None, None, D_MODEL, 2 * D_FF), lambda i: (layer, k, 0, 0), pipeline_mode=resident),
            pl.BlockSpec((None, None, D_FF, D_MODEL), lambda i: (layer, k, 0, 0), pipeline_mode=resident),
            pl.BlockSpec((1, D_MODEL), lambda i: (0, 0)),
            pl.BlockSpec((1, D_MODEL), lambda i: (0, 0)),
        ],
        out_specs=pl.BlockSpec((FFN_TM, D_MODEL), lambda i: (i, 0)),
        out_shape=jax.ShapeDtypeStruct((n, D_MODEL), F32),
        compiler_params=_cparams(("parallel",)),
        name="ffn_ln",
    )(x, w_gu, w_down, g, b)


def _memkv_kernel(mem_ref, wk_ref, wvt_ref, ones_ref, km_ref, vt_ref):
    mb = mem_ref[0].astype(BF16)
    k = _dot(mb, wk_ref[...]) * (MEM_HEAD_DIM ** -0.5)
    lane = lax.broadcasted_iota(jnp.int32, k.shape, 1)
    for h in range(MEM_HEADS):
        in_head = (lane >= h * MEM_HEAD_DIM) & (lane < (h + 1) * MEM_HEAD_DIM)
        km_ref[0, h] = jnp.where(in_head, k, 0.0).astype(BF16)
    vt_ref[0] = (_dot_nt(wvt_ref[...], mb) + ones_ref[...]).astype(BF16)


def _mem_kv(mem, w_k, w_vt, ones_col):
    bsz, m, _ = mem.shape
    vrows = MEM_HEADS * VROWS
    return pl.pallas_call(
        _memkv_kernel,
        grid=(bsz,),
        in_specs=[
            pl.BlockSpec((1, m, D_MODEL), lambda b: (b, 0, 0)),
            pl.BlockSpec((D_MODEL, MEM_WIDTH), lambda b: (0, 0)),
            pl.BlockSpec((vrows, D_MODEL), lambda b: (0, 0)),
            pl.BlockSpec((vrows, 1), lambda b: (0, 0)),
        ],
        out_specs=[
            pl.BlockSpec((1, MEM_HEADS, m, MEM_WIDTH), lambda b: (b, 0, 0, 0)),
            pl.BlockSpec((1, vrows, m), lambda b: (b, 0, 0)),
        ],
        out_shape=[
            jax.ShapeDtypeStruct((bsz, MEM_HEADS, m, MEM_WIDTH), BF16),
            jax.ShapeDtypeStruct((bsz, vrows, m), BF16),
        ],
        compiler_params=_cparams(("parallel",)),
        name="mem_kv",
    )(mem, w_k, w_vt, ones_col)


OUT_TM = 1024


def _mix_out_kernel(x_ref, ot_ref, qm_ref, km_ref, vt_ref, wt_ref, wm_ref, g_ref, b_ref, o_ref):
    qm = qm_ref[...]
    heads = []
    for h in range(MEM_HEADS):
        s = _dot_nt(km_ref[0, h], qm)
        p = jnp.exp2(((s - jnp.max(s, 0, keepdims=True)) * LOG2E).astype(BF16))
        heads.append(_normalized(_dot(vt_ref[0, h * VROWS:(h + 1) * VROWS, :], p)))
    o_mem = jnp.concatenate(heads, 0).T
    mix = _dot(ot_ref[...], wt_ref[...]) + _dot(o_mem.astype(BF16), wm_ref[...])
    y = DN_ALPHA * x_ref[...] + mix
    o_ref[...] = _layer_norm(y, g_ref[...], b_ref[...])


def _mix_out(x, o_tok, q_mem, km, vmt, w_tok, w_mem, g, b, seq):
    n = x.shape[0]
    kt = o_tok.shape[1]
    m = vmt.shape[2]
    per_b = seq // OUT_TM
    return pl.pallas_call(
        _mix_out_kernel,
        grid=(n // OUT_TM,),
        in_specs=[
            pl.BlockSpec((OUT_TM, D_MODEL), lambda i: (i, 0)),
            pl.BlockSpec((OUT_TM, kt), lambda i: (i, 0)),
            pl.BlockSpec((OUT_TM, MEM_WIDTH), lambda i: (i, 0)),
            pl.BlockSpec((1, MEM_HEADS, m, MEM_WIDTH), lambda i: (i // per_b, 0, 0, 0)),
            pl.BlockSpec((1, MEM_HEADS * VROWS, m), lambda i: (i // per_b, 0, 0)),
            pl.BlockSpec((kt, D_MODEL), lambda i: (0, 0)),
            pl.BlockSpec((MEM_WIDTH, D_MODEL), lambda i: (0, 0)),
            pl.BlockSpec((1, D_MODEL), lambda i: (0, 0)),
            pl.BlockSpec((1, D_MODEL), lambda i: (0, 0)),
        ],
        out_specs=pl.BlockSpec((OUT_TM, D_MODEL), lambda i: (i, 0)),
        out_shape=jax.ShapeDtypeStruct((n, D_MODEL), F32),
        compiler_params=_cparams(("parallel",)),
        name="mix_out",
    )(x, o_tok, q_mem, km, vmt, w_tok, w_mem, g, b)


MLA_TM = 1024
MLA_IN_COLS = MLA_Q_RANK + MLA_KV_RANK + 2 * LANES + MEM_WIDTH
HQ = TOK_HEADS * LANES


def _mla_proj_kernel(x_ref, win_ref, qg_ref, kvg_ref, wq_ref, wk_ref, wvt_ref, cos_ref, sin_ref,
                     q_ref, k_ref, vt_ref, qm_ref):
    xb = x_ref[...].astype(BF16)
    hh = _dot(xb, win_ref[...])
    c_q = hh[:, :MLA_Q_RANK]
    c_kv = hh[:, MLA_Q_RANK:MLA_Q_RANK + MLA_KV_RANK]
    o = MLA_Q_RANK + MLA_KV_RANK
    kr = hh[:, o:o + LANES]
    kr_sw = hh[:, o + LANES:o + 2 * LANES]
    qm_ref[...] = hh[:, o + 2 * LANES:].astype(BF16)
    cos = cos_ref[...]
    sin = sin_ref[...]
    qq = _dot(_rms_norm(c_q, qg_ref[...]).astype(BF16), wq_ref[...])
    ckv = _rms_norm(c_kv, kvg_ref[...]).astype(BF16)
    kk = _dot(ckv, wk_ref[...])
    vt_ref[0] = _dot_nt(wvt_ref[...], ckv).astype(BF16)
    kr_rot = kr * cos + kr_sw * sin
    for h in range(TOK_HEADS):
        sl = slice(h * LANES, (h + 1) * LANES)
        sl2 = slice(HQ + h * LANES, HQ + (h + 1) * LANES)
        q_ref[:, sl] = (qq[:, sl] * cos + qq[:, sl2] * sin).astype(BF16)
        k_ref[:, sl] = (kk[:, sl] + kr_rot).astype(BF16)


def _mla_proj(x, w_in, qg, kvg, w_q, w_k, w_vt, cos_t, sin_t, seq):
    n = x.shape[0]
    per_b = seq // MLA_TM
    full = lambda shape: pl.BlockSpec(shape, lambda i: (0, 0))
    return pl.pallas_call(
        _mla_proj_kernel,
        grid=(n // MLA_TM,),
        in_specs=[
            pl.BlockSpec((MLA_TM, D_MODEL), lambda i: (i, 0)),
            full((D_MODEL, MLA_IN_COLS)),
            full((1, MLA_Q_RANK)),
            full((1, MLA_KV_RANK)),
            full((MLA_Q_RANK, 2 * HQ)),
            full((MLA_KV_RANK, HQ)),
            full((TOK_WIDTH, MLA_KV_RANK)),
            pl.BlockSpec((MLA_TM, LANES), lambda i: (i % per_b, 0)),
            pl.BlockSpec((MLA_TM, LANES), lambda i: (i % per_b, 0)),
        ],
        out_specs=[
            pl.BlockSpec((MLA_TM, HQ), lambda i: (i, 0)),
            pl.BlockSpec((MLA_TM, HQ), lambda i: (i, 0)),
            pl.BlockSpec((1, TOK_WIDTH, MLA_TM), lambda i: (i // per_b, 0, i % per_b)),
            pl.BlockSpec((MLA_TM, MEM_WIDTH), lambda i: (i, 0)),
        ],
        out_shape=[
            jax.ShapeDtypeStruct((n, HQ), BF16),
            jax.ShapeDtypeStruct((n, HQ), BF16),
            jax.ShapeDtypeStruct((n // seq, TOK_WIDTH, seq), BF16),
            jax.ShapeDtypeStruct((n, MEM_WIDTH), BF16),
        ],
        compiler_params=_cparams(("parallel",)),
        name="mla_proj",
    )(x, w_in, qg, kvg, w_q, w_k, w_vt, cos_t, sin_t)


MLA_TQ = 512
MLA_TK = 512
MLA_SCALE = (MLA_NOPE + MLA_ROPE) ** -0.5
MLA_EXP2_SCALE = MLA_SCALE * math.log2(math.e)


def _mla_attn_kernel(q_ref, k_ref, vt_ref, o_ref, m_ref, l_ref, acc_ref, s0_ref, s1_ref):
    i = pl.program_id(2)
    m_ref[...] = jnp.full(m_ref.shape, NEG, F32)
    l_ref[...] = jnp.zeros(l_ref.shape, F32)
    acc_ref[...] = jnp.zeros(acc_ref.shape, F32)
    s_refs = (s0_ref, s1_ref)

    def scores(j, hd):
        start = pl.multiple_of(j * MLA_TK, MLA_TK)
        sl = slice(hd * LANES, (hd + 1) * LANES)
        s_refs[hd][...] = _dot_nt(k_ref[pl.ds(start, MLA_TK), sl], q_ref[:, sl])

    def update(j, hd, masked):
        start = pl.multiple_of(j * MLA_TK, MLA_TK)
        rows = slice(hd * HEAD_DIM, (hd + 1) * HEAD_DIM)
        s = s_refs[hd][...]
        if masked:
            krow = lax.broadcasted_iota(jnp.int32, (MLA_TK, MLA_TQ), 0)
            qcol = lax.broadcasted_iota(jnp.int32, (MLA_TK, MLA_TQ), 1)
            s = jnp.where(krow <= qcol, s, NEG)
        m_old = m_ref[hd]
        m_new = jnp.maximum(m_old, jnp.max(s, 0, keepdims=True))
        alpha = jnp.exp2((m_old - m_new) * MLA_EXP2_SCALE)
        p = jnp.exp2((s - m_new) * MLA_EXP2_SCALE)
        l_ref[hd] = alpha * l_ref[hd] + jnp.sum(p, 0, keepdims=True)
        acc_ref[rows, :] = alpha * acc_ref[rows, :] + _dot(vt_ref[0, rows, pl.ds(start, MLA_TK)], p.astype(BF16))
        m_ref[hd] = m_new

    scores(0, 0)

    def tile(j):
        scores(j, 1)
        update(j, 0, False)
        scores(j + 1, 0)
        update(j, 1, False)

    def four_tiles(jj, carry):
        for u in range(4):
            tile(4 * jj + u)
        return carry

    lax.fori_loop(0, i // 4, four_tiles, 0)
    rem = i % 4

    @pl.when(rem >= 2)
    def _():
        tile(i - rem)
        tile(i - rem + 1)

    @pl.when(rem % 2 == 1)
    def _():
        tile(i - 1)

    scores(i, 1)
    update(i, 0, True)
    update(i, 1, True)
    inv = jnp.concatenate(
        [jnp.broadcast_to(1.0 / jnp.maximum(l_ref[hd], 1e-30), (HEAD_DIM, MLA_TQ)) for hd in range(2)], 0)
    o_ref[...] = (acc_ref[...] * inv).T.astype(BF16)


def _normalized(acc, any_valid=None):
    inv = 1.0 / jnp.maximum(acc[HEAD_DIM:HEAD_DIM + 1, :], 1e-30)
    if any_valid is not None:
        inv = jnp.where(any_valid, inv, 0.0)
    return acc[:HEAD_DIM, :] * inv


def _mla_attn(q, k, vt, bsz, seq):
    assert MLA_TQ == MLA_TK
    nq = seq // MLA_TQ
    return pl.pallas_call(
        _mla_attn_kernel,
        grid=(bsz, TOK_HEADS // 2, nq),
        in_specs=[
            pl.BlockSpec((MLA_TQ, 2 * LANES), lambda b, p, i: (b * nq + i, p)),
            pl.BlockSpec((seq, 2 * LANES), lambda b, p, i: (b, p)),
            pl.BlockSpec((1, 2 * HEAD_DIM, seq), lambda b, p, i: (b, p, 0)),
        ],
        out_specs=pl.BlockSpec((MLA_TQ, LANES), lambda b, p, i: (b * nq + i, p)),
        out_shape=jax.ShapeDtypeStruct((bsz * seq, TOK_WIDTH), BF16),
        scratch_shapes=[
            pltpu.VMEM((2, 1, MLA_TQ), F32),
            pltpu.VMEM((2, 1, MLA_TQ), F32),
            pltpu.VMEM((2 * HEAD_DIM, MLA_TQ), F32),
            pltpu.VMEM((MLA_TK, MLA_TQ), F32),
            pltpu.VMEM((MLA_TK, MLA_TQ), F32),
        ],
        compiler_params=_cparams(("parallel", "parallel", "arbitrary")),
        name="mla_attn",
    )(q, k, vt)


KV_TM = 1024
NSA_CMP_COLS = 4 * LANES
KSX = 2 * LANES
NSA_KS_COLS = NSA_GROUPS * KSX
NSA_KW_COLS = NSA_GROUPS * LANES
NSA_KV_COLS = NSA_CMP_COLS + NSA_KS_COLS + NSA_KW_COLS


def _nsa_kv_kernel(x_ref, w_ref, wvt_ref, ones_ref, kaux_ref, kwaux_ref, zc_ref, ks_ref, kw_ref, vst_ref, vwt_ref):
    xb = x_ref[...].astype(BF16)
    y = _dot(xb, w_ref[...])
    for c in range(4):
        zc_ref[c] = y[:, c * LANES:c * LANES + HEAD_DIM]
    o = NSA_CMP_COLS
    kaux = kaux_ref[...]
    kwaux = kwaux_ref[...]
    for g in range(NSA_GROUPS):
        ks_ref[:, g * KSX:(g + 1) * KSX] = (y[:, o + g * KSX:o + (g + 1) * KSX] + kaux).astype(BF16)
        ow = o + NSA_KS_COLS + g * LANES
        kw_ref[:, g * LANES:(g + 1) * LANES] = (y[:, ow:ow + LANES] + kwaux).astype(BF16)
    vt = (_dot_nt(wvt_ref[...], xb) + ones_ref[...]).astype(BF16)
    vst_ref[0] = vt[:NSA_GROUPS * VROWS]
    vwt_ref[0] = vt[NSA_GROUPS * VROWS:]


def _nsa_kv(x, w, wvt, ones_col, kaux, kwaux, seq):
    n = x.shape[0]
    per_b = seq // KV_TM
    vrows = NSA_GROUPS * VROWS
    tile_t = lambda: pl.BlockSpec((1, vrows, KV_TM), lambda i: (i // per_b, 0, i % per_b))
    return pl.pallas_call(
        _nsa_kv_kernel,
        grid=(n // KV_TM,),
        in_specs=[
            pl.BlockSpec((KV_TM, D_MODEL), lambda i: (i, 0)),
            pl.BlockSpec((D_MODEL, NSA_KV_COLS), lambda i: (0, 0)),
            pl.BlockSpec((2 * vrows, D_MODEL), lambda i: (0, 0)),
            pl.BlockSpec((2 * vrows, 1), lambda i: (0, 0)),
            pl.BlockSpec((KV_TM, KSX), lambda i: (i % per_b, 0)),
            pl.BlockSpec((KV_TM, LANES), lambda i: (i % per_b, 0)),
        ],
        out_specs=[
            pl.BlockSpec((4, KV_TM, HEAD_DIM), lambda i: (0, i, 0)),
            pl.BlockSpec((KV_TM, NSA_KS_COLS), lambda i: (i, 0)),
            pl.BlockSpec((KV_TM, NSA_KW_COLS), lambda i: (i, 0)),
            tile_t(),
            tile_t(),
        ],
        out_shape=[
            jax.ShapeDtypeStruct((4, n, HEAD_DIM), F32),
            jax.ShapeDtypeStruct((n, NSA_KS_COLS), BF16),
            jax.ShapeDtypeStruct((n, NSA_KW_COLS), BF16),
            jax.ShapeDtypeStruct((n // seq, vrows, seq), BF16),
            jax.ShapeDtypeStruct((n // seq, vrows, seq), BF16),
        ],
        compiler_params=_cparams(("parallel",)),
        name="nsa_kv",
    )(x, w, wvt, ones_col, kaux, kwaux)


CMP_HALF = CMP_STRIDE * HEAD_DIM


def _compress_kernel(z_ref, pos_ref, w1_ref, b1_ref, w2_ref, o_ref):
    r = z_ref[0]
    rows = r.shape[0]
    lo = _dot((r + pos_ref[0, 0:1, :]).astype(BF16), w1_ref[0, :CMP_HALF, :])
    hi = _dot((r + pos_ref[0, 1:2, :]).astype(BF16), w1_ref[0, CMP_HALF:, :])
    pre = lo + pltpu.roll(hi, rows - 1, 0) + b1_ref[0]
    o_ref[0] = _dot(jax.nn.gelu(pre).astype(BF16), w2_ref[0]).astype(BF16)


def _compress(z, pos, w1, b1, w2, bsz):
    nb, rows, _ = z.shape
    per_kv = NSA_GROUPS * bsz
    return pl.pallas_call(
        _compress_kernel,
        grid=(nb,),
        in_specs=[
            pl.BlockSpec((1, rows, CMP_HALF), lambda i: (i, 0, 0)),
            pl.BlockSpec((1, 2, CMP_HALF), lambda i: (i // per_kv, 0, 0)),
            pl.BlockSpec((1, 2 * CMP_HALF, CMP_HIDDEN), lambda i: (i // per_kv, 0, 0)),
            pl.BlockSpec((1, 1, CMP_HIDDEN), lambda i: (i // per_kv, 0, 0)),
            pl.BlockSpec((1, CMP_HIDDEN, HEAD_DIM), lambda i: (i // per_kv, 0, 0)),
        ],
        out_specs=pl.BlockSpec((1, rows, HEAD_DIM), lambda i: (i, 0, 0)),
        out_shape=jax.ShapeDtypeStruct((nb, rows, HEAD_DIM), BF16),
        compiler_params=_cparams(("parallel",)),
        name="nsa_compress",
    )(z, pos, w1, b1, w2)


NSA_TM = 1024
NSA_GATE_COLS = NSA_GROUPS * LANES
NSA_IN_COLS = HQ + NSA_GATE_COLS + MEM_WIDTH


def _nsa_proj_kernel(x_ref, w_ref, qaux_ref, q_ref, gt_ref, qm_ref):
    y = _dot(x_ref[...].astype(BF16), w_ref[...])
    q_ref[...] = (y[:, :HQ] + qaux_ref[...]).astype(BF16)
    gt_ref[...] = jax.nn.sigmoid(y[:, HQ:HQ + NSA_GATE_COLS])
    qm_ref[...] = y[:, HQ + NSA_GATE_COLS:].astype(BF16)


def _nsa_proj(x, w, qaux):
    n = x.shape[0]
    return pl.pallas_call(
        _nsa_proj_kernel,
        grid=(n // NSA_TM,),
        in_specs=[
            pl.BlockSpec((NSA_TM, D_MODEL), lambda i: (i, 0)),
            pl.BlockSpec((D_MODEL, NSA_IN_COLS), lambda i: (0, 0)),
            pl.BlockSpec((1, HQ), lambda i: (0, 0)),
        ],
        out_specs=[
            pl.BlockSpec((NSA_TM, HQ), lambda i: (i, 0)),
            pl.BlockSpec((NSA_TM, NSA_GATE_COLS), lambda i: (i, 0)),
            pl.BlockSpec((NSA_TM, MEM_WIDTH), lambda i: (i, 0)),
        ],
        out_shape=[
            jax.ShapeDtypeStruct((n, HQ), BF16),
            jax.ShapeDtypeStruct((n, NSA_GATE_COLS), F32),
            jax.ShapeDtypeStruct((n, MEM_WIDTH), BF16),
        ],
        compiler_params=_cparams(("parallel",)),
        name="nsa_proj",
    )(x, w, qaux)


NSA_TQ = 256
NSA_TK = 512
NSA_COLS = NSA_HPG * NSA_TQ


def _nsa_attn_kernel(slopes_ref, q_ref, gt_ref, kc_ref, vca_ref, ks_ref, vst_ref,
                     kw0_ref, kw1_ref, kw2_ref, vwt0_ref, vwt1_ref, vwt2_ref, whi_ref, o_ref,
                     qs_ref, m_ref, acc_ref, oc_ref, ow_ref, sa_ref, sb_ref, *, n_cmp):
    g = pl.program_id(1)
    i = pl.program_id(2)
    t0 = i * NSA_TQ
    jmax = (t0 + NSA_TQ - 1) // NSA_TK
    hcols = lambda hh: slice(hh * NSA_TQ, (hh + 1) * NSA_TQ)
    slope = [slopes_ref[g * NSA_HPG + hh] for hh in range(NSA_HPG)]

    for hh in range(NSA_HPG):
        qs_ref[hcols(hh), :LANES] = q_ref[:, hh * LANES:(hh + 1) * LANES]
    qs = qs_ref[:, :LANES]

    def exp_keys(s, bias):
        s = s + bias
        return jnp.exp2(((s - jnp.max(s, 0, keepdims=True)) * LOG2E).astype(BF16))

    nc = kc_ref.shape[2]
    blk_c = lax.broadcasted_iota(jnp.int32, (nc, NSA_TQ), 0)
    tq_c = t0 + lax.broadcasted_iota(jnp.int32, (nc, NSA_TQ), 1)
    keep_c = (blk_c * CMP_STRIDE + (CMP_LEN - 1) <= tq_c) & (blk_c < n_cmp)
    bias_c = jnp.where(keep_c, 0.0, NEG)
    valid_c = t0 + lax.broadcasted_iota(jnp.int32, (1, NSA_TQ), 1) >= CMP_LEN - 1
    s_all = _dot_nt(kc_ref[0, 0], qs)
    et = jnp.concatenate([exp_keys(s_all[:, hcols(hh)], bias_c) for hh in range(NSA_HPG)], 1)
    oc = _dot(vca_ref[0, 0], et)
    oc_ref[...] = oc[:VROWS]
    imp = jnp.zeros((LANES, NSA_TQ), F32)
    for hh in range(NSA_HPG):
        inv = jnp.where(valid_c, 1.0 / jnp.maximum(oc[HEAD_DIM:HEAD_DIM + 1, hcols(hh)], 1e-30), 0.0)
        imp = imp + oc[VROWS:, hcols(hh)] * inv

    blk = lax.broadcasted_iota(jnp.int32, (LANES, NSA_TQ), 0)
    tq_s = t0 + lax.broadcasted_iota(jnp.int32, (LANES, NSA_TQ), 1)
    cur = tq_s >> SEL_SHIFT
    forced = (blk == 0) | (blk == cur) | (blk == cur - 1)
    imp = jnp.where(forced, imp + FORCE_BONUS, imp)
    imp = jnp.where(blk * SEL_LEN <= tq_s, imp, NEG)
    blk_f = blk.astype(F32)
    selb = jnp.full((LANES, NSA_TQ), NEG, F32)
    for _ in range(SEL_TOPK):
        mx = jnp.max(imp, 0, keepdims=True)
        first = jnp.min(jnp.where(imp == mx, blk_f, float(LANES)), 0, keepdims=True)
        hit = blk_f == first
        selb = jnp.where(hit, 0.0, selb)
        imp = jnp.where(hit, -jnp.inf, imp)
    selb_q = selb.T.astype(BF16)
    for hh in range(NSA_HPG):
        qs_ref[hcols(hh), LANES:] = selb_q

    kw = jnp.concatenate([kw0_ref[...], kw1_ref[...], kw2_ref[...]], 0) + whi_ref[...]
    vwt = jnp.concatenate([vwt0_ref[0], vwt1_ref[0], vwt2_ref[0]], 1)
    pos_w = t0 - 2 * NSA_TQ + lax.broadcasted_iota(jnp.int32, (3 * NSA_TQ, NSA_TQ), 0)
    tq_w = t0 + lax.broadcasted_iota(jnp.int32, (3 * NSA_TQ, NSA_TQ), 1)
    dist_w = tq_w - pos_w
    bias_w = jnp.where((dist_w >= 0) & (dist_w < WINDOW) & (pos_w >= 0), 0.0, NEG)
    s_all = _dot_nt(kw, qs)
    et = jnp.concatenate([exp_keys(s_all[:, hcols(hh)], bias_w) for hh in range(NSA_HPG)], 1)
    ow_ref[...] = _dot(vwt, et)

    m_ref[...] = jnp.full(m_ref.shape, NEG, F32)
    acc_ref[...] = jnp.zeros(acc_ref.shape, F32)
    s_refs = (sa_ref, sb_ref)
    half = NSA_HPG // 2
    tq_row = (t0 + lax.broadcasted_iota(jnp.int32, (1, NSA_TQ), 1)).astype(F32)

    def scores(j, st):
        start = pl.multiple_of(j * NSA_TK, NSA_TK)
        q_rows = slice(st * half * NSA_TQ, (st + 1) * half * NSA_TQ)
        s_refs[st][...] = _dot_nt(ks_ref[pl.ds(start, NSA_TK), :], qs_ref[q_rows, :])

    def update(j, st, diag):
        start = pl.multiple_of(j * NSA_TK, NSA_TK)
        if diag:
            pos_k = j * NSA_TK + lax.broadcasted_iota(jnp.int32, (NSA_TK, NSA_TQ), 0)
            tq_k = t0 + lax.broadcasted_iota(jnp.int32, (NSA_TK, NSA_TQ), 1)
            keep = pos_k <= tq_k
        rel = tq_row - (j * NSA_TK).astype(F32)
        p_list = []
        for hl in range(half):
            hh = st * half + hl
            c = hcols(hh)
            s = s_refs[st][:, hcols(hl)]
            if diag:
                s = jnp.where(keep, s, NEG)
            shift = slope[hh] * rel
            m_old = m_ref[:, c]
            m_new = jnp.maximum(m_old, jnp.max(s, 0, keepdims=True) - shift)
            acc_ref[:, c] = jnp.exp(m_old - m_new) * acc_ref[:, c]
            m_ref[:, c] = m_new
            p_list.append(jnp.exp2(((s - (m_new + shift)) * LOG2E).astype(BF16)))
        cs = slice(st * half * NSA_TQ, (st + 1) * half * NSA_TQ)
        acc_ref[:, cs] += _dot(vst_ref[0, :, pl.ds(start, NSA_TK)], jnp.concatenate(p_list, 1))

    scores(0, 0)

    def tile(j):
        scores(j, 1)
        update(j, 0, False)
        scores(j + 1, 0)
        update(j, 1, False)

    def two_tiles(jj, carry):
        tile(2 * jj)
        tile(2 * jj + 1)
        return carry

    lax.fori_loop(0, jmax // 2, two_tiles, 0)

    @pl.when(jmax % 2 == 1)
    def _():
        tile(jmax - 1)

    scores(jmax, 1)
    update(jmax, 0, True)
    update(jmax, 1, True)

    gtt = gt_ref[...].T
    outs = []
    for hh in range(NSA_HPG):
        c = hcols(hh)
        r = hh * NSA_BRANCHES
        outs.append(gtt[r:r + 1] * _normalized(oc_ref[:, c], valid_c)
                    + gtt[r + 1:r + 2] * _normalized(acc_ref[:, c])
                    + gtt[r + 2:r + 3] * _normalized(ow_ref[:, c]))
    for pr in range(NSA_HPG // 2):
        o_ref[:, pr * LANES:(pr + 1) * LANES] = jnp.concatenate(outs[2 * pr:2 * pr + 2], 0).T.astype(BF16)


def _nsa_attn(slopes, q, gates, kc, vca, ks, vst, kw, vwt, bsz, seq):
    assert WINDOW <= 2 * NSA_TQ and NSA_HPG % 2 == 0
    nq = seq // NSA_TQ
    nc = kc.shape[2]
    n_cmp = (seq - CMP_LEN) // CMP_STRIDE + 1
    gw = NSA_HPG * LANES
    ow = NSA_HPG * HEAD_DIM
    kw_spec = lambda d: pl.BlockSpec((NSA_TQ, LANES), lambda b, g, i: (b * nq + jnp.maximum(i - d, 0), g))
    vwt_spec = lambda d: pl.BlockSpec((1, VROWS, NSA_TQ), lambda b, g, i: (b, g, jnp.maximum(i - d, 0)))
    return pl.pallas_call(
        functools.partial(_nsa_attn_kernel, n_cmp=n_cmp),
        grid=(bsz, NSA_GROUPS, nq),
        in_specs=[
            pl.BlockSpec(memory_space=pltpu.SMEM),
            pl.BlockSpec((NSA_TQ, gw), lambda b, g, i: (b * nq + i, g)),
            pl.BlockSpec((NSA_TQ, LANES), lambda b, g, i: (b * nq + i, g)),
            pl.BlockSpec((1, 1, nc, LANES), lambda b, g, i: (b, g, 0, 0)),
            pl.BlockSpec((1, 1, VROWS + LANES, nc), lambda b, g, i: (b, g, 0, 0)),
            pl.BlockSpec((seq, KSX), lambda b, g, i: (b, g)),
            pl.BlockSpec((1, VROWS, seq), lambda b, g, i: (b, g, 0)),
            kw_spec(2), kw_spec(1), kw_spec(0),
            vwt_spec(2), vwt_spec(1), vwt_spec(0),
            pl.BlockSpec((3 * NSA_TQ, LANES), lambda b, g, i: (0, 0)),
        ],
        out_specs=pl.BlockSpec((NSA_TQ, ow), lambda b, g, i: (b * nq + i, g)),
        out_shape=jax.ShapeDtypeStruct((bsz * seq, TOK_WIDTH), BF16),
        scratch_shapes=[
            pltpu.VMEM((NSA_COLS, KSX), BF16),
            pltpu.VMEM((1, NSA_COLS), F32),
            pltpu.VMEM((VROWS, NSA_COLS), F32),
            pltpu.VMEM((VROWS, NSA_COLS), F32),
            pltpu.VMEM((VROWS, NSA_COLS), F32),
            pltpu.VMEM((NSA_TK, NSA_COLS // 2), F32),
            pltpu.VMEM((NSA_TK, NSA_COLS // 2), F32),
        ],
        compiler_params=_cparams(("parallel", "parallel", "arbitrary")),
        name="nsa_attn",
    )(slopes, q, gates, kc, vca, ks, vst, kw, kw, kw, vwt, vwt, vwt, _nsa_window_offsets())


def _alibi_slopes(n):
    def pow2(k):
        start = 2.0 ** (-8.0 / k)
        return [start ** (i + 1) for i in range(k)]
    if math.log2(n).is_integer():
        s = pow2(n)
    else:
        c = 2 ** math.floor(math.log2(n))
        s = pow2(c) + pow2(2 * c)[0::2][: n - c]
    return np.asarray(s, np.float32)


def _head_slots(w, width):
    r = w.shape[0]
    w3 = w.reshape(r, TOK_HEADS, width)
    return jnp.pad(w3, ((0, 0), (0, 0), (0, LANES - width))).reshape(r, HQ)


def _vt_rows(w):
    n, _, r = w.shape
    return jnp.pad(w, ((0, 0), (0, VROWS - HEAD_DIM), (0, 0))).reshape(n * VROWS, r)


def _ones_rows(n):
    col = np.zeros((n, VROWS, 1), np.float32)
    col[:, HEAD_DIM] = 1.0
    return jnp.asarray(col.reshape(n * VROWS, 1))


def _rope_pair(w_x1, w_x2):
    half = MLA_ROPE // 2
    r = w_x1.shape[0]
    z_lo = jnp.zeros((r, MLA_NOPE), w_x1.dtype)
    z_hi = jnp.zeros((r, LANES - MLA_NOPE - MLA_ROPE), w_x1.dtype)
    assert w_x1.shape[1] == half
    return (jnp.concatenate([z_lo, w_x1, w_x2, z_hi], 1),
            jnp.concatenate([z_lo, -w_x2, w_x1, z_hi], 1))


def _mla_weights(w_in, w_uq, w_ukv):
    half = MLA_ROPE // 2
    o = MLA_Q_RANK + MLA_KV_RANK
    kr, kr_sw = _rope_pair(w_in[:, o:o + half], w_in[:, o + half:o + MLA_ROPE])
    w_in_all = jnp.concatenate([w_in[:, :o], kr, kr_sw, w_in[:, o + MLA_ROPE:]], 1)
    wq3 = w_uq.reshape(MLA_Q_RANK, TOK_HEADS, MLA_NOPE + MLA_ROPE)
    nope, x1, x2 = wq3[..., :MLA_NOPE], wq3[..., MLA_NOPE:MLA_NOPE + half], wq3[..., MLA_NOPE + half:]
    z = jnp.zeros((MLA_Q_RANK, TOK_HEADS, LANES - MLA_NOPE - MLA_ROPE), w_uq.dtype)
    wq = jnp.concatenate([nope, x1, x2, z], -1).reshape(MLA_Q_RANK, HQ)
    wq_sw = jnp.concatenate([jnp.zeros_like(nope), -x2, x1, z], -1).reshape(MLA_Q_RANK, HQ)
    wkv3 = w_ukv.reshape(MLA_KV_RANK, TOK_HEADS, MLA_NOPE + HEAD_DIM)
    wk = _head_slots(wkv3[..., :MLA_NOPE].reshape(MLA_KV_RANK, -1), MLA_NOPE)
    wvt = wkv3[..., MLA_NOPE:].reshape(MLA_KV_RANK, TOK_WIDTH).T
    return (w_in_all.astype(BF16), jnp.concatenate([wq, wq_sw], 1).astype(BF16),
            wk.astype(BF16), wvt.astype(BF16))


def _rope_tables(seq):
    half = MLA_ROPE // 2
    freq = ROPE_THETA ** (-jnp.arange(half, dtype=F32) / half)
    ang = jnp.arange(seq).astype(F32)[:, None] * freq[None, :]
    cos, sin = jnp.cos(ang), jnp.sin(ang)
    ones = jnp.ones((seq, MLA_NOPE), F32)
    z_hi = jnp.zeros((seq, LANES - MLA_NOPE - MLA_ROPE), F32)
    cos_t = jnp.concatenate([ones, cos, cos, z_hi], 1)
    sin_t = jnp.concatenate([jnp.zeros_like(ones), sin, sin, z_hi], 1)
    return cos_t, sin_t


def _bf16_parts(v):
    v = np.asarray(v, np.float32)
    hi = v.astype(BF16).astype(np.float32)
    mid = (v - hi).astype(BF16).astype(np.float32)
    lo = (v - hi - mid).astype(BF16).astype(np.float32)
    return hi, mid, lo


ALIBI_LANE = HEAD_DIM


def _nsa_q_aux():
    row = np.zeros((TOK_HEADS, LANES), np.float32)
    parts = np.stack(_bf16_parts(_alibi_slopes(TOK_HEADS)), 1)
    row[:, ALIBI_LANE:ALIBI_LANE + 3] = parts
    row[:, ALIBI_LANE + 3:ALIBI_LANE + 6] = parts
    return jnp.asarray(row.reshape(1, HQ))


def _nsa_k_aux(seq):
    pos = np.arange(seq)
    rel = pos % NSA_TK
    aux = np.zeros((seq, KSX), np.float32)
    aux[:, ALIBI_LANE:ALIBI_LANE + 3] = (rel // 256 * 256)[:, None]
    aux[:, ALIBI_LANE + 3:ALIBI_LANE + 6] = (rel % 256)[:, None]
    aux[pos, LANES + pos // SEL_LEN] = 1.0
    return jnp.asarray(aux)


def _nsa_kw_aux(seq):
    aux = np.zeros((seq, LANES), np.float32)
    aux[:, ALIBI_LANE + 3:ALIBI_LANE + 6] = (np.arange(seq) % NSA_TQ)[:, None]
    return jnp.asarray(aux)


def _nsa_window_offsets():
    off = np.zeros((3 * NSA_TQ, LANES), np.float32)
    off[:, ALIBI_LANE:ALIBI_LANE + 3] = (np.arange(3 * NSA_TQ) // NSA_TQ * NSA_TQ)[:, None]
    return jnp.asarray(off, BF16)


def _nsa_kc_aux(rows):
    n = np.arange(rows)
    aux = np.zeros((rows, LANES), np.float32)
    aux[:, ALIBI_LANE:ALIBI_LANE + 3] = (n // 256 * 256 * CMP_STRIDE)[:, None]
    aux[:, ALIBI_LANE + 3:ALIBI_LANE + 6] = (n % 256 * CMP_STRIDE)[:, None]
    return jnp.asarray(aux, BF16)


def _nsa_in_weights(w_in):
    scale = HEAD_DIM ** -0.5
    wq = _head_slots(w_in[:, :TOK_WIDTH] * scale, HEAD_DIM)
    ng = NSA_HPG * NSA_BRANCHES
    wg = w_in[:, TOK_WIDTH:TOK_WIDTH + TOK_HEADS * NSA_BRANCHES].reshape(-1, NSA_GROUPS, ng)
    wg = jnp.pad(wg, ((0, 0), (0, 0), (0, LANES - ng))).reshape(-1, NSA_GATE_COLS)
    return jnp.concatenate([wq, wg, w_in[:, TOK_WIDTH + TOK_HEADS * NSA_BRANCHES:]], 1).astype(BF16)


def _nsa_kv_weights(w_kv):
    w5 = w_kv.reshape(D_MODEL, NSA_BRANCHES, 2, NSA_GROUPS * HEAD_DIM)
    cmp_cols = []
    for kv in range(2):
        for g in range(NSA_GROUPS):
            c = w5[:, 0, kv, g * HEAD_DIM:(g + 1) * HEAD_DIM]
            cmp_cols.append(jnp.pad(c, ((0, 0), (0, LANES - HEAD_DIM))))
    group_cols = lambda w, g, width: jnp.pad(w[:, g * HEAD_DIM:(g + 1) * HEAD_DIM], ((0, 0), (0, width - HEAD_DIM)))
    ks_cols = [group_cols(w5[:, 1, 0], g, KSX) for g in range(NSA_GROUPS)]
    kw_cols = [group_cols(w5[:, 2, 0], g, LANES) for g in range(NSA_GROUPS)]
    w = jnp.concatenate(cmp_cols + ks_cols + kw_cols, 1)
    wv = jnp.stack([w5[:, 1, 1], w5[:, 2, 1]], 0).reshape(2, D_MODEL, NSA_GROUPS, HEAD_DIM)
    wvt = _vt_rows(jnp.transpose(wv, (0, 2, 3, 1)).reshape(2 * NSA_GROUPS, HEAD_DIM, D_MODEL))
    return w.astype(BF16), wvt.astype(BF16)


def _agg_matrix(seq, rows):
    n_cmp = (seq - CMP_LEN) // CMP_STRIDE + 1
    n_sel = seq // SEL_LEN
    cmp_start = np.arange(n_cmp) * CMP_STRIDE
    sel_start = np.arange(n_sel) * SEL_LEN
    overlap = np.clip(np.minimum(cmp_start[:, None] + CMP_LEN, sel_start[None, :] + SEL_LEN)
                      - np.maximum(cmp_start[:, None], sel_start[None, :]), 0, None)
    agg_t = np.zeros((LANES, rows), np.float32)
    agg_t[:n_sel, :n_cmp] = (overlap / CMP_LEN).T
    return jnp.asarray(agg_t, BF16)


def kernel(x, mem, ln_g, ln_b, ffn_w_gu, ffn_w_down, w_mem_kv, w_out, mla_w_in, mla_q_norm_g, mla_kv_norm_g,
           mla_w_uq, mla_w_ukv, nsa_w_in, nsa_w_kv, cmp_pos, cmp_w1, cmp_b1, cmp_w2):
    bsz, seq, _ = x.shape
    n = bsz * seq
    assert seq % NSA_TK == 0 and seq % MLA_TQ == 0 and seq // SEL_LEN <= LANES
    h = x.reshape(n, D_MODEL)
    cos_t, sin_t = _rope_tables(seq)
    slopes = jnp.asarray(_alibi_slopes(TOK_HEADS))
    rows_c = seq // CMP_STRIDE
    agg_t = _agg_matrix(seq, rows_c)
    ln = lambda layer, k: (ln_g[layer, k][None, :], ln_b[layer, k][None, :])
    w_gu_bf = ffn_w_gu.astype(BF16)
    w_down_bf = ffn_w_down.astype(BF16)
    shared = None

    for layer in range(DEPTH):
        h = _ffn_ln(h, w_gu_bf, w_down_bf, layer, 0, *ln(layer, 0))
        w_mv = w_mem_kv[layer, :, MEM_WIDTH:].reshape(D_MODEL, MEM_HEADS, MEM_HEAD_DIM)
        km, vm = _mem_kv(mem, w_mem_kv[layer, :, :MEM_WIDTH].astype(BF16),
                         _vt_rows(jnp.transpose(w_mv, (1, 2, 0))).astype(BF16), _ones_rows(MEM_HEADS))
        w_tok = w_out[layer, :TOK_WIDTH].astype(BF16)
        w_memo = w_out[layer, TOK_WIDTH:].astype(BF16)
        if layer < N_A_LAYERS:
            w_in_all, wq, wk, wvt = _mla_weights(mla_w_in[layer], mla_w_uq[layer], mla_w_ukv[layer])
            q, k, vt, q_mem = _mla_proj(h, w_in_all, mla_q_norm_g[layer][None, :], mla_kv_norm_g[layer][None, :],
                                        wq, wk, wvt, cos_t, sin_t, seq)
            o_tok = _mla_attn(q, k, vt, bsz, seq)
        else:
            q, gates, q_mem = _nsa_proj(h, _nsa_in_weights(nsa_w_in[layer - N_A_LAYERS]), _nsa_q_aux())
            o_tok = _nsa_attn(slopes, q, gates, *shared, bsz, seq)
        h = _mix_out(h, o_tok, q_mem, km, vm, w_tok, w_memo, *ln(layer, 1), seq)
        h = _ffn_ln(h, w_gu_bf, w_down_bf, layer, 1, *ln(layer, 2))
        if layer == N_A_LAYERS - 1:
            zc, ks, kw, vst, vwt = _nsa_kv(h, *_nsa_kv_weights(nsa_w_kv), _ones_rows(2 * NSA_GROUPS),
                                           _nsa_k_aux(seq), _nsa_kw_aux(seq), seq)
            z = zc.reshape(4 * bsz, rows_c, CMP_HALF)
            c = _compress(z, cmp_pos.reshape(2, 2, CMP_HALF), cmp_w1.astype(BF16), cmp_b1[:, None, :],
                          cmp_w2.astype(BF16), bsz)
            c = c.reshape(2, NSA_GROUPS, bsz, rows_c, HEAD_DIM)
            kc = jnp.pad(jnp.swapaxes(c[0], 0, 1), ((0, 0), (0, 0), (0, 0), (0, LANES - HEAD_DIM)))
            kc = kc + _nsa_kc_aux(rows_c)
            vct = jnp.pad(jnp.swapaxes(c[1], 2, 3), ((0, 0), (0, 0), (0, VROWS - HEAD_DIM), (0, 0)))
            vct = vct.at[:, :, HEAD_DIM, :].set(1.0)
            agg_b = jnp.broadcast_to(agg_t, (NSA_GROUPS, bsz) + agg_t.shape)
            vca = jnp.swapaxes(jnp.concatenate([vct, agg_b], 2), 0, 1)
            shared = (kc, vca, ks, vst, kw, vwt)
    return h.reshape(bsz, seq, D_MODEL)
```

```python
import functools
import math

import numpy as np
import jax
import jax.numpy as jnp
from jax import lax
from jax.experimental import pallas as pl
from jax.experimental.pallas import tpu as pltpu

F32 = jnp.float32
BF16 = jnp.bfloat16

D_MODEL = 1024
DEPTH = 4
N_A_LAYERS = DEPTH // 2

TOK_HEADS = 12
HEAD_DIM = 64
MEM_HEADS = 4
MEM_HEAD_DIM = 64
TOK_WIDTH = TOK_HEADS * HEAD_DIM
MEM_WIDTH = MEM_HEADS * MEM_HEAD_DIM

MLA_Q_RANK = 256
MLA_KV_RANK = 128
MLA_NOPE = 64
MLA_ROPE = 32
ROPE_THETA = 10000.0

NSA_GROUPS = 2
NSA_HPG = TOK_HEADS // NSA_GROUPS
NSA_BRANCHES = 3
CMP_LEN = 32
CMP_STRIDE = 16
CMP_HIDDEN = 256
SEL_LEN = 64
SEL_SHIFT = 6
SEL_TOPK = 16
WINDOW = 512

D_FF = 2816
DN_ALPHA = (2 * DEPTH) ** 0.25
LN_EPS = 1e-5
RMS_EPS = 1e-6
NEG = -1e30
FORCE_BONUS = 1e4
LOG2E = math.log2(math.e)

LANES = 128
VROWS = HEAD_DIM + 16
VMEM_LIMIT = 56 * 1024 * 1024


def _cparams(sem):
    return pltpu.CompilerParams(dimension_semantics=sem, vmem_limit_bytes=VMEM_LIMIT)


def _layer_norm(y, g, b):
    mu = jnp.mean(y, -1, keepdims=True)
    yc = y - mu
    var = jnp.mean(yc * yc, -1, keepdims=True)
    return yc * lax.rsqrt(var + LN_EPS) * g + b


def _rms_norm(x, g):
    return x * lax.rsqrt(jnp.mean(x * x, -1, keepdims=True) + RMS_EPS) * g


def _dot(a, b):
    return jnp.dot(a, b, preferred_element_type=F32)


def _dot_nt(a, b):
    return lax.dot_general(a, b, (((1,), (1,)), ((), ())), preferred_element_type=F32)


FFN_TM = 1024
FFN_SPLIT = 1536


def _ffn_kernel(x_ref, wgu_ref, wd_ref, g_ref, b_ref, o_ref):
    x = x_ref[...]
    xb = x.astype(BF16)
    part = None
    for lo, hi in ((0, FFN_SPLIT), (FFN_SPLIT, D_FF)):
        gate = _dot(xb, wgu_ref[:, lo:hi])
        up = _dot(xb, wgu_ref[:, D_FF + lo:D_FF + hi])
        h = (gate * jax.nn.sigmoid(gate) * up).astype(BF16)
        d = _dot(h, wd_ref[lo:hi, :])
        part = d if part is None else part + d
    y = DN_ALPHA * x + 0.5 * part
    o_ref[...] = _layer_norm(y, g_ref[...], b_ref[...])


def _ffn_ln(x, w_gu, w_down, layer, k, g, b):
    n = x.shape[0]
    resident = pl.Buffered(1)
    return pl.pallas_call(
        _ffn_kernel,
        grid=(n // FFN_TM,),
        in_specs=[
            pl.BlockSpec((FFN_TM, D_MODEL), lambda i: (i, 0)),
            pl.BlockSpec((None, None, D_MODEL, 2 * D_FF), lambda i: (layer, k, 0, 0), pipeline_mode=resident),
            pl.BlockSpec((None, None, D_FF, D_MODEL), lambda i: (layer, k, 0, 0), pipeline_mode=resident),
            pl.BlockSpec((1, D_MODEL), lambda i: (0, 0)),
            pl.BlockSpec((1, D_MODEL), lambda i: (0, 0)),
        ],
        out_specs=pl.BlockSpec((FFN_TM, D_MODEL), lambda i: (i, 0)),
        out_shape=jax.ShapeDtypeStruct((n, D_MODEL), F32),
        compiler_params=_cparams(("parallel",)),
        name="ffn_ln",
    )(x, w_gu, w_down, g, b)


def _memkv_kernel(mem_ref, wk_ref, wvt_ref, ones_ref, km_ref, vt_ref):
    mb = mem_ref[0].astype(BF16)
    k = _dot(mb, wk_ref[...]) * (MEM_HEAD_DIM ** -0.5)
    lane = lax.broadcasted_iota(jnp.int32, k.shape, 1)
    for h in range(MEM_HEADS):
        in_head = (lane >= h * MEM_HEAD_DIM) & (lane < (h + 1) * MEM_HEAD_DIM)
        km_ref[0, h] = jnp.where(in_head, k, 0.0).astype(BF16)
    vt_ref[0] = (_dot_nt(wvt_ref[...], mb) + ones_ref[...]).astype(BF16)


def _mem_kv(mem, w_k, w_vt, ones_col):
    bsz, m, _ = mem.shape
    vrows = MEM_HEADS * VROWS
    return pl.pallas_call(
        _memkv_kernel,
        grid=(bsz,),
        in_specs=[
            pl.BlockSpec((1, m, D_MODEL), lambda b: (b, 0, 0)),
            pl.BlockSpec((D_MODEL, MEM_WIDTH), lambda b: (0, 0)),
            pl.BlockSpec((vrows, D_MODEL), lambda b: (0, 0)),
            pl.BlockSpec((vrows, 1), lambda b: (0, 0)),
        ],
        out_specs=[
            pl.BlockSpec((1, MEM_HEADS, m, MEM_WIDTH), lambda b: (b, 0, 0, 0)),
            pl.BlockSpec((1, vrows, m), lambda b: (b, 0, 0)),
        ],
        out_shape=[
            jax.ShapeDtypeStruct((bsz, MEM_HEADS, m, MEM_WIDTH), BF16),
            jax.ShapeDtypeStruct((bsz, vrows, m), BF16),
        ],
        compiler_params=_cparams(("parallel",)),
        name="mem_kv",
    )(mem, w_k, w_vt, ones_col)


OUT_TM = 1024


def _mix_out_kernel(x_ref, ot_ref, qm_ref, km_ref, vt_ref, wt_ref, wm_ref, g_ref, b_ref, o_ref):
    qm = qm_ref[...]
    heads = []
    for h in range(MEM_HEADS):
        s = _dot_nt(km_ref[0, h], qm)
        p = jnp.exp2(((s - jnp.max(s, 0, keepdims=True)) * LOG2E).astype(BF16))
        heads.append(_normalized(_dot(vt_ref[0, h * VROWS:(h + 1) * VROWS, :], p)))
    o_mem = jnp.concatenate(heads, 0).T
    mix = _dot(ot_ref[...], wt_ref[...]) + _dot(o_mem.astype(BF16), wm_ref[...])
    y = DN_ALPHA * x_ref[...] + mix
    o_ref[...] = _layer_norm(y, g_ref[...], b_ref[...])


def _mix_out(x, o_tok, q_mem, km, vmt, w_tok, w_mem, g, b, seq):
    n = x.shape[0]
    kt = o_tok.shape[1]
    m = vmt.shape[2]
    per_b = seq // OUT_TM
    return pl.pallas_call(
        _mix_out_kernel,
        grid=(n // OUT_TM,),
        in_specs=[
            pl.BlockSpec((OUT_TM, D_MODEL), lambda i: (i, 0)),
            pl.BlockSpec((OUT_TM, kt), lambda i: (i, 0)),
            pl.BlockSpec((OUT_TM, MEM_WIDTH), lambda i: (i, 0)),
            pl.BlockSpec((1, MEM_HEADS, m, MEM_WIDTH), lambda i: (i // per_b, 0, 0, 0)),
            pl.BlockSpec((1, MEM_HEADS * VROWS, m), lambda i: (i // per_b, 0, 0)),
            pl.BlockSpec((kt, D_MODEL), lambda i: (0, 0)),
            pl.BlockSpec((MEM_WIDTH, D_MODEL), lambda i: (0, 0)),
            pl.BlockSpec((1, D_MODEL), lambda i: (0, 0)),
            pl.BlockSpec((1, D_MODEL), lambda i: (0, 0)),
        ],
        out_specs=pl.BlockSpec((OUT_TM, D_MODEL), lambda i: (i, 0)),
        out_shape=jax.ShapeDtypeStruct((n, D_MODEL), F32),
        compiler_params=_cparams(("parallel",)),
        name="mix_out",
    )(x, o_tok, q_mem, km, vmt, w_tok, w_mem, g, b)


MLA_TM = 1024
MLA_IN_COLS = MLA_Q_RANK + MLA_KV_RANK + 2 * LANES + MEM_WIDTH
HQ = TOK_HEADS * LANES


def _mla_proj_kernel(x_ref, win_ref, qg_ref, kvg_ref, wq_ref, wk_ref, wvt_ref, cos_ref, sin_ref,
                     q_ref, k_ref, vt_ref, qm_ref):
    xb = x_ref[...].astype(BF16)
    hh = _dot(xb, win_ref[...])
    c_q = hh[:, :MLA_Q_RANK]
    c_kv = hh[:, MLA_Q_RANK:MLA_Q_RANK + MLA_KV_RANK]
    o = MLA_Q_RANK + MLA_KV_RANK
    kr = hh[:, o:o + LANES]
    kr_sw = hh[:, o + LANES:o + 2 * LANES]
    qm_ref[...] = hh[:, o + 2 * LANES:].astype(BF16)
    cos = cos_ref[...]
    sin = sin_ref[...]
    qq = _dot(_rms_norm(c_q, qg_ref[...]).astype(BF16), wq_ref[...])
    ckv = _rms_norm(c_kv, kvg_ref[...]).astype(BF16)
    kk = _dot(ckv, wk_ref[...])
    vt_ref[0] = _dot_nt(wvt_ref[...], ckv).astype(BF16)
    kr_rot = kr * cos + kr_sw * sin
    for h in range(TOK_HEADS):
        sl = slice(h * LANES, (h + 1) * LANES)
        sl2 = slice(HQ + h * LANES, HQ + (h + 1) * LANES)
        q_ref[:, sl] = (qq[:, sl] * cos + qq[:, sl2] * sin).astype(BF16)
        k_ref[:, sl] = (kk[:, sl] + kr_rot).astype(BF16)


def _mla_proj(x, w_in, qg, kvg, w_q, w_k, w_vt, cos_t, sin_t, seq):
    n = x.shape[0]
    per_b = seq // MLA_TM
    full = lambda shape: pl.BlockSpec(shape, lambda i: (0, 0))
    return pl.pallas_call(
        _mla_proj_kernel,
        grid=(n // MLA_TM,),
        in_specs=[
            pl.BlockSpec((MLA_TM, D_MODEL), lambda i: (i, 0)),
            full((D_MODEL, MLA_IN_COLS)),
            full((1, MLA_Q_RANK)),
            full((1, MLA_KV_RANK)),
            full((MLA_Q_RANK, 2 * HQ)),
            full((MLA_KV_RANK, HQ)),
            full((TOK_WIDTH, MLA_KV_RANK)),
            pl.BlockSpec((MLA_TM, LANES), lambda i: (i % per_b, 0)),
            pl.BlockSpec((MLA_TM, LANES), lambda i: (i % per_b, 0)),
        ],
        out_specs=[
            pl.BlockSpec((MLA_TM, HQ), lambda i: (i, 0)),
            pl.BlockSpec((MLA_TM, HQ), lambda i: (i, 0)),
            pl.BlockSpec((1, TOK_WIDTH, MLA_TM), lambda i: (i // per_b, 0, i % per_b)),
            pl.BlockSpec((MLA_TM, MEM_WIDTH), lambda i: (i, 0)),
        ],
        out_shape=[
            jax.ShapeDtypeStruct((n, HQ), BF16),
            jax.ShapeDtypeStruct((n, HQ), BF16),
            jax.ShapeDtypeStruct((n // seq, TOK_WIDTH, seq), BF16),
            jax.ShapeDtypeStruct((n, MEM_WIDTH), BF16),
        ],
        compiler_params=_cparams(("parallel",)),
        name="mla_proj",
    )(x, w_in, qg, kvg, w_q, w_k, w_vt, cos_t, sin_t)


MLA_TQ = 512
MLA_TK = 512
MLA_SCALE = (MLA_NOPE + MLA_ROPE) ** -0.5
MLA_EXP2_SCALE = MLA_SCALE * math.log2(math.e)


def _mla_attn_kernel(q_ref, k_ref, vt_ref, o_ref, m_ref, l_ref, acc_ref, s0_ref, s1_ref):
    i = pl.program_id(2)
    m_ref[...] = jnp.full(m_ref.shape, NEG, F32)
    l_ref[...] = jnp.zeros(l_ref.shape, F32)
    acc_ref[...] = jnp.zeros(acc_ref.shape, F32)
    s_refs = (s0_ref, s1_ref)

    def scores(j, hd):
        start = pl.multiple_of(j * MLA_TK, MLA_TK)
        sl = slice(hd * LANES, (hd + 1) * LANES)
        s_refs[hd][...] = _dot_nt(k_ref[pl.ds(start, MLA_TK), sl], q_ref[:, sl])

    def update(j, hd, masked):
        start = pl.multiple_of(j * MLA_TK, MLA_TK)
        rows = slice(hd * HEAD_DIM, (hd + 1) * HEAD_DIM)
        s = s_refs[hd][...]
        if masked:
            krow = lax.broadcasted_iota(jnp.int32, (MLA_TK, MLA_TQ), 0)
            qcol = lax.broadcasted_iota(jnp.int32, (MLA_TK, MLA_TQ), 1)
            s = jnp.where(krow <= qcol, s, NEG)
        m_old = m_ref[hd]
        m_new = jnp.maximum(m_old, jnp.max(s, 0, keepdims=True))
        alpha = jnp.exp2((m_old - m_new) * MLA_EXP2_SCALE)
        p = jnp.exp2((s - m_new) * MLA_EXP2_SCALE)
        l_ref[hd] = alpha * l_ref[hd] + jnp.sum(p, 0, keepdims=True)
        acc_ref[rows, :] = alpha * acc_ref[rows, :] + _dot(vt_ref[0, rows, pl.ds(start, MLA_TK)], p.astype(BF16))
        m_ref[hd] = m_new

    scores(0, 0)

    def tile(j):
        scores(j, 1)
        update(j, 0, False)
        scores(j + 1, 0)
        update(j, 1, False)

    def four_tiles(jj, carry):
        for u in range(4):
            tile(4 * jj + u)
        return carry

    lax.fori_loop(0, i // 4, four_tiles, 0)
    rem = i % 4

    @pl.when(rem >= 2)
    def _():
        tile(i - rem)
        tile(i - rem + 1)

    @pl.when(rem % 2 == 1)
    def _():
        tile(i - 1)

    scores(i, 1)
    update(i, 0, True)
    update(i, 1, True)
    inv = jnp.concatenate(
        [jnp.broadcast_to(1.0 / jnp.maximum(l_ref[hd], 1e-30), (HEAD_DIM, MLA_TQ)) for hd in range(2)], 0)
    o_ref[...] = (acc_ref[...] * inv).T.astype(BF16)


def _normalized(acc, any_valid=None):
    inv = 1.0 / jnp.maximum(acc[HEAD_DIM:HEAD_DIM + 1, :], 1e-30)
    if any_valid is not None:
        inv = jnp.where(any_valid, inv, 0.0)
    return acc[:HEAD_DIM, :] * inv


def _mla_attn(q, k, vt, bsz, seq):
    assert MLA_TQ == MLA_TK
    nq = seq // MLA_TQ
    return pl.pallas_call(
        _mla_attn_kernel,
        grid=(bsz, TOK_HEADS // 2, nq),
        in_specs=[
            pl.BlockSpec((MLA_TQ, 2 * LANES), lambda b, p, i: (b * nq + i, p)),
            pl.BlockSpec((seq, 2 * LANES), lambda b, p, i: (b, p)),
            pl.BlockSpec((1, 2 * HEAD_DIM, seq), lambda b, p, i: (b, p, 0)),
        ],
        out_specs=pl.BlockSpec((MLA_TQ, LANES), lambda b, p, i: (b * nq + i, p)),
        out_shape=jax.ShapeDtypeStruct((bsz * seq, TOK_WIDTH), BF16),
        scratch_shapes=[
            pltpu.VMEM((2, 1, MLA_TQ), F32),
            pltpu.VMEM((2, 1, MLA_TQ), F32),
            pltpu.VMEM((2 * HEAD_DIM, MLA_TQ), F32),
            pltpu.VMEM((MLA_TK, MLA_TQ), F32),
            pltpu.VMEM((MLA_TK, MLA_TQ), F32),
        ],
        compiler_params=_cparams(("parallel", "parallel", "arbitrary")),
        name="mla_attn",
    )(q, k, vt)


KV_TM = 1024
NSA_CMP_COLS = 4 * LANES
KSX = 2 * LANES
NSA_KS_COLS = NSA_GROUPS * KSX
NSA_KW_COLS = NSA_GROUPS * LANES
NSA_KV_COLS = NSA_CMP_COLS + NSA_KS_COLS + NSA_KW_COLS


def _nsa_kv_kernel(x_ref, w_ref, wvt_ref, ones_ref, kaux_ref, kwaux_ref, zc_ref, ks_ref, kw_ref, vst_ref, vwt_ref):
    xb = x_ref[...].astype(BF16)
    y = _dot(xb, w_ref[...])
    for c in range(4):
        zc_ref[c] = y[:, c * LANES:c * LANES + HEAD_DIM]
    o = NSA_CMP_COLS
    kaux = kaux_ref[...]
    kwaux = kwaux_ref[...]
    for g in range(NSA_GROUPS):
        ks_ref[:, g * KSX:(g + 1) * KSX] = (y[:, o + g * KSX:o + (g + 1) * KSX] + kaux).astype(BF16)
        ow = o + NSA_KS_COLS + g * LANES
        kw_ref[:, g * LANES:(g + 1) * LANES] = (y[:, ow:ow + LANES] + kwaux).astype(BF16)
    vt = (_dot_nt(wvt_ref[...], xb) + ones_ref[...]).astype(BF16)
    vst_ref[0] = vt[:NSA_GROUPS * VROWS]
    vwt_ref[0] = vt[NSA_GROUPS * VROWS:]


def _nsa_kv(x, w, wvt, ones_col, kaux, kwaux, seq):
    n = x.shape[0]
    per_b = seq // KV_TM
    vrows = NSA_GROUPS * VROWS
    tile_t = lambda: pl.BlockSpec((1, vrows, KV_TM), lambda i: (i // per_b, 0, i % per_b))
    return pl.pallas_call(
        _nsa_kv_kernel,
        grid=(n // KV_TM,),
        in_specs=[
            pl.BlockSpec((KV_TM, D_MODEL), lambda i: (i, 0)),
            pl.BlockSpec((D_MODEL, NSA_KV_COLS), lambda i: (0, 0)),
            pl.BlockSpec((2 * vrows, D_MODEL), lambda i: (0, 0)),
            pl.BlockSpec((2 * vrows, 1), lambda i: (0, 0)),
            pl.BlockSpec((KV_TM, KSX), lambda i: (i % per_b, 0)),
            pl.BlockSpec((KV_TM, LANES), lambda i: (i % per_b, 0)),
        ],
        out_specs=[
            pl.BlockSpec((4, KV_TM, HEAD_DIM), lambda i: (0, i, 0)),
            pl.BlockSpec((KV_TM, NSA_KS_COLS), lambda i: (i, 0)),
            pl.BlockSpec((KV_TM, NSA_KW_COLS), lambda i: (i, 0)),
            tile_t(),
            tile_t(),
        ],
        out_shape=[
            jax.ShapeDtypeStruct((4, n, HEAD_DIM), F32),
            jax.ShapeDtypeStruct((n, NSA_KS_COLS), BF16),
            jax.ShapeDtypeStruct((n, NSA_KW_COLS), BF16),
            jax.ShapeDtypeStruct((n // seq, vrows, seq), BF16),
            jax.ShapeDtypeStruct((n // seq, vrows, seq), BF16),
        ],
        compiler_params=_cparams(("parallel",)),
        name="nsa_kv",
    )(x, w, wvt, ones_col, kaux, kwaux)


CMP_HALF = CMP_STRIDE * HEAD_DIM


def _compress_kernel(z_ref, pos_ref, w1_ref, b1_ref, w2_ref, o_ref):
    r = z_ref[0]
    rows = r.shape[0]
    lo = _dot((r + pos_ref[0, 0:1, :]).astype(BF16), w1_ref[0, :CMP_HALF, :])
    hi = _dot((r + pos_ref[0, 1:2, :]).astype(BF16), w1_ref[0, CMP_HALF:, :])
    pre = lo + pltpu.roll(hi, rows - 1, 0) + b1_ref[0]
    o_ref[0] = _dot(jax.nn.gelu(pre).astype(BF16), w2_ref[0]).astype(BF16)


def _compress(z, pos, w1, b1, w2, bsz):
    nb, rows, _ = z.shape
    per_kv = NSA_GROUPS * bsz
    return pl.pallas_call(
        _compress_kernel,
        grid=(nb,),
        in_specs=[
            pl.BlockSpec((1, rows, CMP_HALF), lambda i: (i, 0, 0)),
            pl.BlockSpec((1, 2, CMP_HALF), lambda i: (i // per_kv, 0, 0)),
            pl.BlockSpec((1, 2 * CMP_HALF, CMP_HIDDEN), lambda i: (i // per_kv, 0, 0)),
            pl.BlockSpec((1, 1, CMP_HIDDEN), lambda i: (i // per_kv, 0, 0)),
            pl.BlockSpec((1, CMP_HIDDEN, HEAD_DIM), lambda i: (i // per_kv, 0, 0)),
        ],
        out_specs=pl.BlockSpec((1, rows, HEAD_DIM), lambda i: (i, 0, 0)),
        out_shape=jax.ShapeDtypeStruct((nb, rows, HEAD_DIM), BF16),
        compiler_params=_cparams(("parallel",)),
        name="nsa_compress",
    )(z, pos, w1, b1, w2)


NSA_TM = 1024
NSA_GATE_COLS = NSA_GROUPS * LANES
NSA_IN_COLS = HQ + NSA_GATE_COLS + MEM_WIDTH


def _nsa_proj_kernel(x_ref, w_ref, qaux_ref, q_ref, gt_ref, qm_ref):
    y = _dot(x_ref[...].astype(BF16), w_ref[...])
    q_ref[...] = (y[:, :HQ] + qaux_ref[...]).astype(BF16)
    gt_ref[...] = jax.nn.sigmoid(y[:, HQ:HQ + NSA_GATE_COLS])
    qm_ref[...] = y[:, HQ + NSA_GATE_COLS:].astype(BF16)


def _nsa_proj(x, w, qaux):
    n = x.shape[0]
    return pl.pallas_call(
        _nsa_proj_kernel,
        grid=(n // NSA_TM,),
        in_specs=[
            pl.BlockSpec((NSA_TM, D_MODEL), lambda i: (i, 0)),
            pl.BlockSpec((D_MODEL, NSA_IN_COLS), lambda i: (0, 0)),
            pl.BlockSpec((1, HQ), lambda i: (0, 0)),
        ],
        out_specs=[
            pl.BlockSpec((NSA_TM, HQ), lambda i: (i, 0)),
            pl.BlockSpec((NSA_TM, NSA_GATE_COLS), lambda i: (i, 0)),
            pl.BlockSpec((NSA_TM, MEM_WIDTH), lambda i: (i, 0)),
        ],
        out_shape=[
            jax.ShapeDtypeStruct((n, HQ), BF16),
            jax.ShapeDtypeStruct((n, NSA_GATE_COLS), F32),
            jax.ShapeDtypeStruct((n, MEM_WIDTH), BF16),
        ],
        compiler_params=_cparams(("parallel",)),
        name="nsa_proj",
    )(x, w, qaux)


NSA_TQ = 256
NSA_TK = 512
NSA_COLS = NSA_HPG * NSA_TQ


def _nsa_attn_kernel(slopes_ref, q_ref, gt_ref, kc_ref, vca_ref, ks_ref, vst_ref,
                     kw0_ref, kw1_ref, kw2_ref, vwt0_ref, vwt1_ref, vwt2_ref, whi_ref, o_ref,
                     qs_ref, m_ref, acc_ref, oc_ref, ow_ref, sa_ref, sb_ref, *, n_cmp):
    g = pl.program_id(1)
    i = pl.program_id(2)
    t0 = i * NSA_TQ
    jmax = (t0 + NSA_TQ - 1) // NSA_TK
    hcols = lambda hh: slice(hh * NSA_TQ, (hh + 1) * NSA_TQ)
    slope = [slopes_ref[g * NSA_HPG + hh] for hh in range(NSA_HPG)]

    for hh in range(NSA_HPG):
        qs_ref[hcols(hh), :LANES] = q_ref[:, hh * LANES:(hh + 1) * LANES]
    qs = qs_ref[:, :LANES]

    def exp_keys(s, bias):
        s = s + bias
        return jnp.exp2(((s - jnp.max(s, 0, keepdims=True)) * LOG2E).astype(BF16))

    nc = kc_ref.shape[2]
    blk_c = lax.broadcasted_iota(jnp.int32, (nc, NSA_TQ), 0)
    tq_c = t0 + lax.broadcasted_iota(jnp.int32, (nc, NSA_TQ), 1)
    keep_c = (blk_c * CMP_STRIDE + (CMP_LEN - 1) <= tq_c) & (blk_c < n_cmp)
    bias_c = jnp.where(keep_c, 0.0, NEG)
    valid_c = t0 + lax.broadcasted_iota(jnp.int32, (1, NSA_TQ), 1) >= CMP_LEN - 1
    s_all = _dot_nt(kc_ref[0, 0], qs)
    et = jnp.concatenate([exp_keys(s_all[:, hcols(hh)], bias_c) for hh in range(NSA_HPG)], 1)
    oc = _dot(vca_ref[0, 0], et)
    oc_ref[...] = oc[:VROWS]
    imp = jnp.zeros((LANES, NSA_TQ), F32)
    for hh in range(NSA_HPG):
        inv = jnp.where(valid_c, 1.0 / jnp.maximum(oc[HEAD_DIM:HEAD_DIM + 1, hcols(hh)], 1e-30), 0.0)
        imp = imp + oc[VROWS:, hcols(hh)] * inv

    blk = lax.broadcasted_iota(jnp.int32, (LANES, NSA_TQ), 0)
    tq_s = t0 + lax.broadcasted_iota(jnp.int32, (LANES, NSA_TQ), 1)
    cur = tq_s >> SEL_SHIFT
    forced = (blk == 0) | (blk == cur) | (blk == cur - 1)
    imp = jnp.where(forced, imp + FORCE_BONUS, imp)
    imp = jnp.where(blk * SEL_LEN <= tq_s, imp, NEG)
    blk_f = blk.astype(F32)
    selb = jnp.full((LANES, NSA_TQ), NEG, F32)
    for _ in range(SEL_TOPK):
        mx = jnp.max(imp, 0, keepdims=True)
        first = jnp.min(jnp.where(imp == mx, blk_f, float(LANES)), 0, keepdims=True)
        hit = blk_f == first
        selb = jnp.where(hit, 0.0, selb)
        imp = jnp.where(hit, -jnp.inf, imp)
    selb_q = selb.T.astype(BF16)
    for hh in range(NSA_HPG):
        qs_ref[hcols(hh), LANES:] = selb_q

    kw = jnp.concatenate([kw0_ref[...], kw1_ref[...], kw2_ref[...]], 0) + whi_ref[...]
    vwt = jnp.concatenate([vwt0_ref[0], vwt1_ref[0], vwt2_ref[0]], 1)
    pos_w = t0 - 2 * NSA_TQ + lax.broadcasted_iota(jnp.int32, (3 * NSA_TQ, NSA_TQ), 0)
    tq_w = t0 + lax.broadcasted_iota(jnp.int32, (3 * NSA_TQ, NSA_TQ), 1)
    dist_w = tq_w - pos_w
    bias_w = jnp.where((dist_w >= 0) & (dist_w < WINDOW) & (pos_w >= 0), 0.0, NEG)
    s_all = _dot_nt(kw, qs)
    et = jnp.concatenate([exp_keys(s_all[:, hcols(hh)], bias_w) for hh in range(NSA_HPG)], 1)
    ow_ref[...] = _dot(vwt, et)

    m_ref[...] = jnp.full(m_ref.shape, NEG, F32)
    acc_ref[...] = jnp.zeros(acc_ref.shape, F32)
    s_refs = (sa_ref, sb_ref)
    half = NSA_HPG // 2
    tq_row = (t0 + lax.broadcasted_iota(jnp.int32, (1, NSA_TQ), 1)).astype(F32)

    def scores(j, st):
        start = pl.multiple_of(j * NSA_TK, NSA_TK)
        q_rows = slice(st * half * NSA_TQ, (st + 1) * half * NSA_TQ)
        s_refs[st][...] = _dot_nt(ks_ref[pl.ds(start, NSA_TK), :], qs_ref[q_rows, :])

    def update(j, st, diag):
        start = pl.multiple_of(j * NSA_TK, NSA_TK)
        if diag:
            pos_k = j * NSA_TK + lax.broadcasted_iota(jnp.int32, (NSA_TK, NSA_TQ), 0)
            tq_k = t0 + lax.broadcasted_iota(jnp.int32, (NSA_TK, NSA_TQ), 1)
            keep = pos_k <= tq_k
        rel = tq_row - (j * NSA_TK).astype(F32)
        p_list = []
        for hl in range(half):
            hh = st * half + hl
            c = hcols(hh)
            s = s_refs[st][:, hcols(hl)]
            if diag:
                s = jnp.where(keep, s, NEG)
            shift = slope[hh] * rel
            m_old = m_ref[:, c]
            m_new = jnp.maximum(m_old, jnp.max(s, 0, keepdims=True) - shift)
            acc_ref[:, c] = jnp.exp(m_old - m_new) * acc_ref[:, c]
            m_ref[:, c] = m_new
            p_list.append(jnp.exp2(((s - (m_new + shift)) * LOG2E).astype(BF16)))
        cs = slice(st * half * NSA_TQ, (st + 1) * half * NSA_TQ)
        acc_ref[:, cs] += _dot(vst_ref[0, :, pl.ds(start, NSA_TK)], jnp.concatenate(p_list, 1))

    scores(0, 0)

    def tile(j):
        scores(j, 1)
        update(j, 0, False)
        scores(j + 1, 0)
        update(j, 1, False)

    def four_tiles(jj, carry):
        for u in range(4):
            tile(4 * jj + u)
        return carry

    lax.fori_loop(0, jmax // 4, four_tiles, 0)
    rem = jmax % 4

    @pl.when(rem >= 2)
    def _():
        tile(jmax - rem)
        tile(jmax - rem + 1)

    @pl.when(rem % 2 == 1)
    def _():
        tile(jmax - 1)

    scores(jmax, 1)
    update(jmax, 0, True)
    update(jmax, 1, True)

    gtt = gt_ref[...].T
    outs = []
    for hh in range(NSA_HPG):
        c = hcols(hh)
        r = hh * NSA_BRANCHES
        outs.append(gtt[r:r + 1] * _normalized(oc_ref[:, c], valid_c)
                    + gtt[r + 1:r + 2] * _normalized(acc_ref[:, c])
                    + gtt[r + 2:r + 3] * _normalized(ow_ref[:, c]))
    for pr in range(NSA_HPG // 2):
        o_ref[:, pr * LANES:(pr + 1) * LANES] = jnp.concatenate(outs[2 * pr:2 * pr + 2], 0).T.astype(BF16)


def _nsa_attn(slopes, q, gates, kc, vca, ks, vst, kw, vwt, bsz, seq):
    assert WINDOW <= 2 * NSA_TQ and NSA_HPG % 2 == 0
    nq = seq // NSA_TQ
    nc = kc.shape[2]
    n_cmp = (seq - CMP_LEN) // CMP_STRIDE + 1
    gw = NSA_HPG * LANES
    ow = NSA_HPG * HEAD_DIM
    kw_spec = lambda d: pl.BlockSpec((NSA_TQ, LANES), lambda b, g, i: (b * nq + jnp.maximum(i - d, 0), g))
    vwt_spec = lambda d: pl.BlockSpec((1, VROWS, NSA_TQ), lambda b, g, i: (b, g, jnp.maximum(i - d, 0)))
    return pl.pallas_call(
        functools.partial(_nsa_attn_kernel, n_cmp=n_cmp),
        grid=(bsz, NSA_GROUPS, nq),
        in_specs=[
            pl.BlockSpec(memory_space=pltpu.SMEM),
            pl.BlockSpec((NSA_TQ, gw), lambda b, g, i: (b * nq + i, g)),
            pl.BlockSpec((NSA_TQ, LANES), lambda b, g, i: (b * nq + i, g)),
            pl.BlockSpec((1, 1, nc, LANES), lambda b, g, i: (b, g, 0, 0)),
            pl.BlockSpec((1, 1, VROWS + LANES, nc), lambda b, g, i: (b, g, 0, 0)),
            pl.BlockSpec((seq, KSX), lambda b, g, i: (b, g)),
            pl.BlockSpec((1, VROWS, seq), lambda b, g, i: (b, g, 0)),
            kw_spec(2), kw_spec(1), kw_spec(0),
            vwt_spec(2), vwt_spec(1), vwt_spec(0),
            pl.BlockSpec((3 * NSA_TQ, LANES), lambda b, g, i: (0, 0)),
        ],
        out_specs=pl.BlockSpec((NSA_TQ, ow), lambda b, g, i: (b * nq + i, g)),
        out_shape=jax.ShapeDtypeStruct((bsz * seq, TOK_WIDTH), BF16),
        scratch_shapes=[
            pltpu.VMEM((NSA_COLS, KSX), BF16),
            pltpu.VMEM((1, NSA_COLS), F32),
            pltpu.VMEM((VROWS, NSA_COLS), F32),
            pltpu.VMEM((VROWS, NSA_COLS), F32),
            pltpu.VMEM((VROWS, NSA_COLS), F32),
            pltpu.VMEM((NSA_TK, NSA_COLS // 2), F32),
            pltpu.VMEM((NSA_TK, NSA_COLS // 2), F32),
        ],
        compiler_params=_cparams(("parallel", "parallel", "arbitrary")),
        name="nsa_attn",
    )(slopes, q, gates, kc, vca, ks, vst, kw, kw, kw, vwt, vwt, vwt, _nsa_window_offsets())


def _alibi_slopes(n):
    def pow2(k):
        start = 2.0 ** (-8.0 / k)
        return [start ** (i + 1) for i in range(k)]
    if math.log2(n).is_integer():
        s = pow2(n)
    else:
        c = 2 ** math.floor(math.log2(n))
        s = pow2(c) + pow2(2 * c)[0::2][: n - c]
    return np.asarray(s, np.float32)


def _head_slots(w, width):
    r = w.shape[0]
    w3 = w.reshape(r, TOK_HEADS, width)
    return jnp.pad(w3, ((0, 0), (0, 0), (0, LANES - width))).reshape(r, HQ)


def _vt_rows(w):
    n, _, r = w.shape
    return jnp.pad(w, ((0, 0), (0, VROWS - HEAD_DIM), (0, 0))).reshape(n * VROWS, r)


def _ones_rows(n):
    col = np.zeros((n, VROWS, 1), np.float32)
    col[:, HEAD_DIM] = 1.0
    return jnp.asarray(col.reshape(n * VROWS, 1))


def _rope_pair(w_x1, w_x2):
    half = MLA_ROPE // 2
    r = w_x1.shape[0]
    z_lo = jnp.zeros((r, MLA_NOPE), w_x1.dtype)
    z_hi = jnp.zeros((r, LANES - MLA_NOPE - MLA_ROPE), w_x1.dtype)
    assert w_x1.shape[1] == half
    return (jnp.concatenate([z_lo, w_x1, w_x2, z_hi], 1),
            jnp.concatenate([z_lo, -w_x2, w_x1, z_hi], 1))


def _mla_weights(w_in, w_uq, w_ukv):
    half = MLA_ROPE // 2
    o = MLA_Q_RANK + MLA_KV_RANK
    kr, kr_sw = _rope_pair(w_in[:, o:o + half], w_in[:, o + half:o + MLA_ROPE])
    w_in_all = jnp.concatenate([w_in[:, :o], kr, kr_sw, w_in[:, o + MLA_ROPE:]], 1)
    wq3 = w_uq.reshape(MLA_Q_RANK, TOK_HEADS, MLA_NOPE + MLA_ROPE)
    nope, x1, x2 = wq3[..., :MLA_NOPE], wq3[..., MLA_NOPE:MLA_NOPE + half], wq3[..., MLA_NOPE + half:]
    z = jnp.zeros((MLA_Q_RANK, TOK_HEADS, LANES - MLA_NOPE - MLA_ROPE), w_uq.dtype)
    wq = jnp.concatenate([nope, x1, x2, z], -1).reshape(MLA_Q_RANK, HQ)
    wq_sw = jnp.concatenate([jnp.zeros_like(nope), -x2, x1, z], -1).reshape(MLA_Q_RANK, HQ)
    wkv3 = w_ukv.reshape(MLA_KV_RANK, TOK_HEADS, MLA_NOPE + HEAD_DIM)
    wk = _head_slots(wkv3[..., :MLA_NOPE].reshape(MLA_KV_RANK, -1), MLA_NOPE)
    wvt = wkv3[..., MLA_NOPE:].reshape(MLA_KV_RANK, TOK_WIDTH).T
    return (w_in_all.astype(BF16), jnp.concatenate([wq, wq_sw], 1).astype(BF16),
            wk.astype(BF16), wvt.astype(BF16))


def _rope_tables(seq):
    half = MLA_ROPE // 2
    freq = ROPE_THETA ** (-jnp.arange(half, dtype=F32) / half)
    ang = jnp.arange(seq).astype(F32)[:, None] * freq[None, :]
    cos, sin = jnp.cos(ang), jnp.sin(ang)
    ones = jnp.ones((seq, MLA_NOPE), F32)
    z_hi = jnp.zeros((seq, LANES - MLA_NOPE - MLA_ROPE), F32)
    cos_t = jnp.concatenate([ones, cos, cos, z_hi], 1)
    sin_t = jnp.concatenate([jnp.zeros_like(ones), sin, sin, z_hi], 1)
    return cos_t, sin_t


def _bf16_parts(v):
    v = np.asarray(v, np.float32)
    hi = v.astype(BF16).astype(np.float32)
    mid = (v - hi).astype(BF16).astype(np.float32)
    lo = (v - hi - mid).astype(BF16).astype(np.float32)
    return hi, mid, lo


ALIBI_LANE = HEAD_DIM


def _nsa_q_aux():
    row = np.zeros((TOK_HEADS, LANES), np.float32)
    parts = np.stack(_bf16_parts(_alibi_slopes(TOK_HEADS)), 1)
    row[:, ALIBI_LANE:ALIBI_LANE + 3] = parts
    row[:, ALIBI_LANE + 3:ALIBI_LANE + 6] = parts
    return jnp.asarray(row.reshape(1, HQ))


def _nsa_k_aux(seq):
    pos = np.arange(seq)
    rel = pos % NSA_TK
    aux = np.zeros((seq, KSX), np.float32)
    aux[:, ALIBI_LANE:ALIBI_LANE + 3] = (rel // 256 * 256)[:, None]
    aux[:, ALIBI_LANE + 3:ALIBI_LANE + 6] = (rel % 256)[:, None]
    aux[pos, LANES + pos // SEL_LEN] = 1.0
    return jnp.asarray(aux)


def _nsa_kw_aux(seq):
    aux = np.zeros((seq, LANES), np.float32)
    aux[:, ALIBI_LANE + 3:ALIBI_LANE + 6] = (np.arange(seq) % NSA_TQ)[:, None]
    return jnp.asarray(aux)


def _nsa_window_offsets():
    off = np.zeros((3 * NSA_TQ, LANES), np.float32)
    off[:, ALIBI_LANE:ALIBI_LANE + 3] = (np.arange(3 * NSA_TQ) // NSA_TQ * NSA_TQ)[:, None]
    return jnp.asarray(off, BF16)


def _nsa_kc_aux(rows):
    n = np.arange(rows)
    aux = np.zeros((rows, LANES), np.float32)
    aux[:, ALIBI_LANE:ALIBI_LANE + 3] = (n // 256 * 256 * CMP_STRIDE)[:, None]
    aux[:, ALIBI_LANE + 3:ALIBI_LANE + 6] = (n % 256 * CMP_STRIDE)[:, None]
    return jnp.asarray(aux, BF16)


def _nsa_in_weights(w_in):
    scale = HEAD_DIM ** -0.5
    wq = _head_slots(w_in[:, :TOK_WIDTH] * scale, HEAD_DIM)
    ng = NSA_HPG * NSA_BRANCHES
    wg = w_in[:, TOK_WIDTH:TOK_WIDTH + TOK_HEADS * NSA_BRANCHES].reshape(-1, NSA_GROUPS, ng)
    wg = jnp.pad(wg, ((0, 0), (0, 0), (0, LANES - ng))).reshape(-1, NSA_GATE_COLS)
    return jnp.concatenate([wq, wg, w_in[:, TOK_WIDTH + TOK_HEADS * NSA_BRANCHES:]], 1).astype(BF16)


def _nsa_kv_weights(w_kv):
    w5 = w_kv.reshape(D_MODEL, NSA_BRANCHES, 2, NSA_GROUPS * HEAD_DIM)
    cmp_cols = []
    for kv in range(2):
        for g in range(NSA_GROUPS):
            c = w5[:, 0, kv, g * HEAD_DIM:(g + 1) * HEAD_DIM]
            cmp_cols.append(jnp.pad(c, ((0, 0), (0, LANES - HEAD_DIM))))
    group_cols = lambda w, g, width: jnp.pad(w[:, g * HEAD_DIM:(g + 1) * HEAD_DIM], ((0, 0), (0, width - HEAD_DIM)))
    ks_cols = [group_cols(w5[:, 1, 0], g, KSX) for g in range(NSA_GROUPS)]
    kw_cols = [group_cols(w5[:, 2, 0], g, LANES) for g in range(NSA_GROUPS)]
    w = jnp.concatenate(cmp_cols + ks_cols + kw_cols, 1)
    wv = jnp.stack([w5[:, 1, 1], w5[:, 2, 1]], 0).reshape(2, D_MODEL, NSA_GROUPS, HEAD_DIM)
    wvt = _vt_rows(jnp.transpose(wv, (0, 2, 3, 1)).reshape(2 * NSA_GROUPS, HEAD_DIM, D_MODEL))
    return w.astype(BF16), wvt.astype(BF16)


def _agg_matrix(seq, rows):
    n_cmp = (seq - CMP_LEN) // CMP_STRIDE + 1
    n_sel = seq // SEL_LEN
    cmp_start = np.arange(n_cmp) * CMP_STRIDE
    sel_start = np.arange(n_sel) * SEL_LEN
    overlap = np.clip(np.minimum(cmp_start[:, None] + CMP_LEN, sel_start[None, :] + SEL_LEN)
                      - np.maximum(cmp_start[:, None], sel_start[None, :]), 0, None)
    agg_t = np.zeros((LANES, rows), np.float32)
    agg_t[:n_sel, :n_cmp] = (overlap / CMP_LEN).T
    return jnp.asarray(agg_t, BF16)


def kernel(x, mem, ln_g, ln_b, ffn_w_gu, ffn_w_down, w_mem_kv, w_out, mla_w_in, mla_q_norm_g, mla_kv_norm_g,
           mla_w_uq, mla_w_ukv, nsa_w_in, nsa_w_kv, cmp_pos, cmp_w1, cmp_b1, cmp_w2):
    bsz, seq, _ = x.shape
    n = bsz * seq
    assert seq % NSA_TK == 0 and seq % MLA_TQ == 0 and seq // SEL_LEN <= LANES
    h = x.reshape(n, D_MODEL)
    cos_t, sin_t = _rope_tables(seq)
    slopes = jnp.asarray(_alibi_slopes(TOK_HEADS))
    rows_c = seq // CMP_STRIDE
    agg_t = _agg_matrix(seq, rows_c)
    ln = lambda layer, k: (ln_g[layer, k][None, :], ln_b[layer, k][None, :])
    w_gu_bf = ffn_w_gu.astype(BF16)
    w_down_bf = ffn_w_down.astype(BF16)
    shared = None

    for layer in range(DEPTH):
        h = _ffn_ln(h, w_gu_bf, w_down_bf, layer, 0, *ln(layer, 0))
        w_mv = w_mem_kv[layer, :, MEM_WIDTH:].reshape(D_MODEL, MEM_HEADS, MEM_HEAD_DIM)
        km, vm = _mem_kv(mem, w_mem_kv[layer, :, :MEM_WIDTH].astype(BF16),
                         _vt_rows(jnp.transpose(w_mv, (1, 2, 0))).astype(BF16), _ones_rows(MEM_HEADS))
        w_tok = w_out[layer, :TOK_WIDTH].astype(BF16)
        w_memo = w_out[layer, TOK_WIDTH:].astype(BF16)
        if layer < N_A_LAYERS:
            w_in_all, wq, wk, wvt = _mla_weights(mla_w_in[layer], mla_w_uq[layer], mla_w_ukv[layer])
            q, k, vt, q_mem = _mla_proj(h, w_in_all, mla_q_norm_g[layer][None, :], mla_kv_norm_g[layer][None, :],
                                        wq, wk, wvt, cos_t, sin_t, seq)
            o_tok = _mla_attn(q, k, vt, bsz, seq)
        else:
            q, gates, q_mem = _nsa_proj(h, _nsa_in_weights(nsa_w_in[layer - N_A_LAYERS]), _nsa_q_aux())
            o_tok = _nsa_attn(slopes, q, gates, *shared, bsz, seq)
        h = _mix_out(h, o_tok, q_mem, km, vm, w_tok, w_memo, *ln(layer, 1), seq)
        h = _ffn_ln(h, w_gu_bf, w_down_bf, layer, 1, *ln(layer, 2))
        if layer == N_A_LAYERS - 1:
            zc, ks, kw, vst, vwt = _nsa_kv(h, *_nsa_kv_weights(nsa_w_kv), _ones_rows(2 * NSA_GROUPS),
                                           _nsa_k_aux(seq), _nsa_kw_aux(seq), seq)
            z = zc.reshape(4 * bsz, rows_c, CMP_HALF)
            c = _compress(z, cmp_pos.reshape(2, 2, CMP_HALF), cmp_w1.astype(BF16), cmp_b1[:, None, :],
                          cmp_w2.astype(BF16), bsz)
            c = c.reshape(2, NSA_GROUPS, bsz, rows_c, HEAD_DIM)
            kc = jnp.pad(jnp.swapaxes(c[0], 0, 1), ((0, 0), (0, 0), (0, 0), (0, LANES - HEAD_DIM)))
            kc = kc + _nsa_kc_aux(rows_c)
            vct = jnp.pad(jnp.swapaxes(c[1], 2, 3), ((0, 0), (0, 0), (0, VROWS - HEAD_DIM), (0, 0)))
            vct = vct.at[:, :, HEAD_DIM, :].set(1.0)
            agg_b = jnp.broadcast_to(agg_t, (NSA_GROUPS, bsz) + agg_t.shape)
            vca = jnp.swapaxes(jnp.concatenate([vct, agg_b], 2), 0, 1)
            shared = (kc, vca, ks, vst, kw, vwt)
    return h.reshape(bsz, seq, D_MODEL)
```
